```python
import math
import jax
import jax.numpy as jnp
from jax import lax
import numpy as np

D_MODEL = 1024
BATCH = 2
SEQ = 8192
DEPTH = 4
DEC_BATCH = 128
DEC_SEQ = 8
PAST_LEN = 2048
PAGE_SIZE = 128

HEAD_DIM = 64
BRANCH_W = D_MODEL // 2
N_BRANCH = 3
NSA_H = BRANCH_W // HEAD_DIM
NSA_KVH = 2
FOX_H = BRANCH_W // HEAD_DIM
FOX_KVH = 4
DIFF_H = BRANCH_W // (2 * HEAD_DIM)
DIFF_KVH = 2
L_CMP = 32
D_CMP = 16
L_SEL = 64
N_SEL = 16
WINDOW = 512
Q_BLOCK = 128
N_BUCKETS = 32
MAX_DISTANCE = 128
D_FF = -(-8 * D_MODEL // (3 * 256)) * 256
N_QK_GAINS = 8
FORGET_BIAS_INIT = 3.0
RMS_EPS = 1e-6
NEG_BIG = -1e30
PROJ_SPLITS = (NSA_H * HEAD_DIM, 6 * NSA_KVH * HEAD_DIM, 3 * NSA_H,
               FOX_H * HEAD_DIM, 2 * FOX_KVH * HEAD_DIM, FOX_H,
               DIFF_H * 2 * HEAD_DIM, 2 * DIFF_KVH * 2 * HEAD_DIM, N_BRANCH * D_MODEL)
D_PROJ = sum(PROJ_SPLITS)

kernel_name = 'hybrid_nsa_fox_diff_decoder_step'


def rms_norm(x, g):
    xf = x.astype(jnp.float32)
    y = xf * lax.rsqrt(jnp.mean(xf * xf, axis=-1, keepdims=True) + RMS_EPS)
    return (y * g.astype(jnp.float32)).astype(x.dtype)


def t5_bucket(dist):
    n = jnp.maximum(dist, 0)
    exact = N_BUCKETS // 2
    nf = jnp.maximum(n, 1).astype(jnp.float32)
    large = exact + (jnp.log(nf / exact) / math.log(MAX_DISTANCE / exact) * (N_BUCKETS - exact)).astype(jnp.int32)
    return jnp.where(n < exact, n, jnp.minimum(large, N_BUCKETS - 1))


def head_bias(tab, dist, g, r):
    b = tab[t5_bucket(dist)].astype(jnp.float32).reshape(dist.shape + (g, r))
    return jnp.transpose(b, (2, 3, 0, 1))


def masked_softmax(s, mask):
    s = jnp.where(mask, s.astype(jnp.float32), NEG_BIG)
    p = jnp.exp(s - jnp.max(s, axis=-1, keepdims=True)) * mask
    return p / jnp.maximum(jnp.sum(p, axis=-1, keepdims=True), 1e-30)


def split_proj(p):
    idx = [int(i) for i in np.cumsum(PROJ_SPLITS)[:-1]]
    return jnp.split(p, idx, axis=-1)


def gather_pages(pool, page_table):
    g = pool[page_table]
    return g.reshape((g.shape[0], g.shape[1] * g.shape[2]) + g.shape[3:])


def sweep(fn, s, b):
    out = lax.map(fn, jnp.arange(s // Q_BLOCK))
    return jnp.moveaxis(out, 0, 1).reshape(b, s, -1)


def dslice(a, t0, n):
    return lax.dynamic_slice_in_dim(a, t0, n, axis=1)


def project(h, w_in_l, b_f_l, qk_l):
    B, T, _ = h.shape
    nq, nkv, ng, fq, fkv, ff, dq, dkv, mg = split_proj(h @ w_in_l)
    nq = rms_norm(nq.reshape(B, T, NSA_KVH, NSA_H // NSA_KVH, HEAD_DIM), qk_l[0])
    nkv = nkv.reshape(B, T, 3, 2, NSA_KVH, HEAD_DIM)
    rows = [jnp.stack([rms_norm(nkv[:, :, i, 0], qk_l[1 + i]), nkv[:, :, i, 1]], axis=2) for i in range(3)]
    ng = jax.nn.sigmoid(ng.reshape(B, T, NSA_KVH, NSA_H // NSA_KVH, 3))
    fq = rms_norm(fq.reshape(B, T, FOX_KVH, FOX_H // FOX_KVH, HEAD_DIM), qk_l[4])
    fkv = fkv.reshape(B, T, 2, FOX_KVH, HEAD_DIM)
    fox_rows = jnp.stack([rms_norm(fkv[:, :, 0], qk_l[5]), fkv[:, :, 1]], axis=2)
    logf = jax.nn.log_sigmoid((ff + b_f_l).astype(jnp.float32))
    dq = rms_norm(dq.reshape(B, T, DIFF_KVH, DIFF_H // DIFF_KVH, 2, HEAD_DIM), qk_l[6])
    dkv = dkv.reshape(B, T, 2, DIFF_KVH, 2, HEAD_DIM)
    diff_rows = jnp.stack([rms_norm(dkv[:, :, 0], qk_l[7]), dkv[:, :, 1]], axis=2).reshape(B, T, 2, DIFF_KVH, 2 * HEAD_DIM)
    mg = jax.nn.sigmoid(mg.reshape(B, T, N_BRANCH, D_MODEL))
    return nq, ng, rows[0], rows[1], rows[2], fq, fox_rows, logf, dq, diff_rows, mg


def nsa_compress(k, w, pe):
    B, L, G, Dh = k.shape
    n_chunk = -(-L // D_CMP)
    ch = jnp.pad(k, ((0, 0), (0, n_chunk * D_CMP - L), (0, 0), (0, 0))).reshape(B, n_chunk, D_CMP, G, Dh)
    first = jnp.einsum('bclgd,lde->bcge', ch + pe[:D_CMP, None, :], w[:D_CMP])
    second = jnp.einsum('bclgd,lde->bcge', ch + pe[D_CMP:, None, :], w[D_CMP:])
    return first[:, :-1] + second[:, 1:]


def nsa_prepare(cmp_rows, sel_rows, cmp_w, cmp_pe):
    kc = nsa_compress(cmp_rows[:, :, 0], cmp_w[0], cmp_pe[0])
    vc = nsa_compress(cmp_rows[:, :, 1], cmp_w[1], cmp_pe[1])
    B, L = sel_rows.shape[:2]
    ns = -(-L // L_SEL)
    sel = jnp.pad(sel_rows, ((0, 0), (0, ns * L_SEL - L), (0, 0), (0, 0), (0, 0)))
    sel = jnp.transpose(sel.reshape(B, ns, L_SEL, 2, NSA_KVH, HEAD_DIM), (3, 0, 4, 1, 2, 5))
    return kc, vc, sel[0], sel[1]


def nsa_block(q, pos_q, kc, vc, ksb, vsb, kw, vw, pos_w, gate, tab):
    B, T, G, R, _ = q.shape
    nc, ns = kc.shape[1], ksb.shape[2]
    scale = HEAD_DIM ** -0.5
    cidx = jnp.arange(nc)
    dist_c = pos_q[:, None] - (cidx * D_CMP + L_CMP - 1)[None, :]
    s_c = jnp.einsum('btgrd,bcgd->bgrtc', q, kc).astype(jnp.float32) * scale + head_bias(tab, dist_c, G, R)
    p_c = masked_softmax(s_c, dist_c >= 0)
    o_c = jnp.einsum('bgrtc,bcgd->btgrd', p_c.astype(vc.dtype), vc)
    sidx = jnp.arange(ns)
    overlap = ((cidx[:, None] * D_CMP < (sidx[None, :] + 1) * L_SEL)
               & (cidx[:, None] * D_CMP + L_CMP > sidx[None, :] * L_SEL)).astype(jnp.float32)
    imp = jnp.einsum('bgrtc,cn->btgn', p_c, overlap)
    cur = (pos_q // L_SEL)[:, None]
    future = (sidx[None, :] > cur)[None, :, None, :]
    forced = ((sidx[None, :] == 0) | (sidx[None, :] == cur) | (sidx[None, :] == cur - 1))[None, :, None, :]
    score = jnp.where(future, -jnp.inf, jnp.where(forced, jnp.inf, imp))
    n_top = min(N_SEL, ns)
    _, sel = lax.top_k(score, n_top)
    bidx = jnp.arange(B)[:, None, None, None]
    gidx = jnp.arange(G)[None, None, :, None]
    m = n_top * L_SEL
    kg = ksb[bidx, gidx, sel].reshape(B, T, G, m, HEAD_DIM)
    vg = vsb[bidx, gidx, sel].reshape(B, T, G, m, HEAD_DIM)
    pos_s = (sel[..., None] * L_SEL + jnp.arange(L_SEL)).reshape(B, T, G, m)
    dist_s = pos_q[None, :, None, None] - pos_s
    tab_g = tab.reshape(N_BUCKETS, G, R).astype(jnp.float32)
    bias_s = tab_g[t5_bucket(dist_s), gidx]
    s_s = jnp.einsum('btgrd,btgmd->btgrm', q, kg).astype(jnp.float32) * scale + jnp.swapaxes(bias_s, -1, -2)
    p_s = masked_softmax(s_s, (dist_s >= 0)[:, :, :, None, :])
    o_s = jnp.einsum('btgrm,btgmd->btgrd', p_s.astype(vg.dtype), vg)
    dist_w = pos_q[:, None] - pos_w[None, :]
    mask_w = (dist_w >= 0) & (dist_w < WINDOW) & (pos_w >= 0)[None, :]
    s_w = jnp.einsum('btgrd,blgd->bgrtl', q, kw).astype(jnp.float32) * scale + head_bias(tab, dist_w, G, R)
    p_w = masked_softmax(s_w, mask_w)
    o_w = jnp.einsum('bgrtl,blgd->btgrd', p_w.astype(vw.dtype), vw)
    return gate[..., 0:1] * o_c + gate[..., 1:2] * o_s + gate[..., 2:3] * o_w


def fox_block(q, c_q, pos_q, k, v, c_k, pos_k):
    s = (jnp.einsum('btgrd,blgd->bgrtl', q, k).astype(jnp.float32) * HEAD_DIM ** -0.5
         + jnp.transpose(c_q, (0, 2, 3, 1))[..., None] - jnp.transpose(c_k, (0, 2, 3, 1))[..., None, :])
    p = masked_softmax(s, pos_q[:, None] >= pos_k[None, :])
    return jnp.einsum('bgrtl,blgd->btgrd', p.astype(v.dtype), v)


def diff_block(q, pos_q, rows, pos_k, lam, tab):
    B, L = rows.shape[:2]
    G, R = q.shape[2], q.shape[3]
    k = rows[:, :, 0].reshape(B, L, G, 2, HEAD_DIM)
    v = rows[:, :, 1]
    dist = pos_q[:, None] - pos_k[None, :]
    mask = dist >= 0
    bias = head_bias(tab, dist, G, R)
    scale = HEAD_DIM ** -0.5
    s1 = jnp.einsum('btgrd,blgd->bgrtl', q[..., 0, :], k[..., 0, :]).astype(jnp.float32) * scale + bias
    s2 = jnp.einsum('btgrd,blgd->bgrtl', q[..., 1, :], k[..., 1, :]).astype(jnp.float32) * scale + bias
    a = masked_softmax(s1, mask) - lam * masked_softmax(s2, mask)
    return jnp.einsum('bgrtl,blgd->btgrd', a.astype(v.dtype), v)


def diff_post(o, g, lam_init):
    B, T = o.shape[:2]
    return (rms_norm(o, g) * (1.0 - lam_init)).reshape(B, T, BRANCH_W)


def merge(o_a, o_b, o_c, mg, w_br, w_o):
    o = jnp.stack([o_a, o_b, o_c], axis=2)
    u = jnp.einsum('btnc,ncd->btnd', o, w_br)
    return jnp.sum(mg * u, axis=2) @ w_o


def swiglu(h, w_gu, w_d):
    g, u = jnp.split(h @ w_gu, 2, axis=-1)
    return (jax.nn.silu(g) * u) @ w_d


def setup_inputs(seed: int = 0):
    key = jax.random.key(seed)
    ks = jax.random.split(key, 24)
    f32 = jnp.float32
    n_pages = PAST_LEN // PAGE_SIZE
    n_pool = (DEC_BATCH * n_pages * 5) // 4
    win_buf = min(WINDOW, PAST_LEN)

    def nrm(k, shape, s=1.0):
        return jax.random.normal(k, shape, f32) * s

    page_table = jax.random.permutation(ks[0], n_pool)[:DEC_BATCH * n_pages].reshape(DEC_BATCH, n_pages).astype(jnp.int32)
    return {
        'x_prompt': nrm(ks[1], (BATCH, SEQ, D_MODEL)),
        'x_sample': nrm(ks[2], (DEC_BATCH, DEC_SEQ, D_MODEL)),
        'cache_nsa_cmp_kv': nrm(ks[3], (DEPTH, n_pool, PAGE_SIZE, 2, NSA_KVH, HEAD_DIM)),
        'cache_nsa_sel_kv': nrm(ks[4], (DEPTH, n_pool, PAGE_SIZE, 2, NSA_KVH, HEAD_DIM)),
        'state_nsa_win_kv': nrm(ks[5], (DEPTH, DEC_BATCH, win_buf, 2, NSA_KVH, HEAD_DIM)),
        'cache_fox_kv': nrm(ks[6], (DEPTH, n_pool, PAGE_SIZE, 2, FOX_KVH, HEAD_DIM)),
        'cache_fox_logf': jax.nn.log_sigmoid(FORGET_BIAS_INIT + nrm(ks[7], (DEPTH, n_pool, PAGE_SIZE, FOX_H))),
        'cache_diff_kv': nrm(ks[8], (DEPTH, n_pool, PAGE_SIZE, 2, DIFF_KVH, 2 * HEAD_DIM)),
        'page_table': page_table,
        'rel_bias_table': nrm(ks[9], (N_BUCKETS, NSA_H + DIFF_H), 0.2),
        'norm_mix': 1.0 + nrm(ks[10], (DEPTH, D_MODEL), 0.01),
        'norm_ffn': 1.0 + nrm(ks[11], (DEPTH, D_MODEL), 0.01),
        'w_in': nrm(ks[12], (DEPTH, D_MODEL, D_PROJ), D_MODEL ** -0.5),
        'b_forget': FORGET_BIAS_INIT + nrm(ks[13], (DEPTH, FOX_H), 0.1),
        'qk_gain': 1.0 + nrm(ks[14], (DEPTH, N_QK_GAINS, HEAD_DIM), 0.01),
        'nsa_cmp_w': nrm(ks[15], (DEPTH, 2, L_CMP, HEAD_DIM, HEAD_DIM), (L_CMP * HEAD_DIM) ** -0.5),
        'nsa_cmp_pe': nrm(ks[16], (DEPTH, 2, L_CMP, HEAD_DIM), 0.1),
        'diff_lambda': nrm(ks[17], (DEPTH, 4, HEAD_DIM), 0.1),
        'diff_subln': 1.0 + nrm(ks[18], (DEPTH, 2 * HEAD_DIM), 0.01),
        'w_branch': nrm(ks[19], (DEPTH, N_BRANCH, BRANCH_W, D_MODEL), BRANCH_W ** -0.5),
        'w_out': nrm(ks[20], (DEPTH, D_MODEL, D_MODEL), D_MODEL ** -0.5),
        'w_gate_up': nrm(ks[21], (DEPTH, D_MODEL, 2 * D_FF), D_MODEL ** -0.5),
        'w_down': nrm(ks[22], (DEPTH, D_FF, D_MODEL), D_FF ** -0.5),
    }


def reference(x_prompt, x_sample, cache_nsa_cmp_kv, cache_nsa_sel_kv, state_nsa_win_kv, cache_fox_kv,
              cache_fox_logf, cache_diff_kv, page_table, rel_bias_table, norm_mix, norm_ffn, w_in, b_forget,
              qk_gain, nsa_cmp_w, nsa_cmp_pe, diff_lambda, diff_subln, w_branch, w_out, w_gate_up, w_down):
    tab_nsa = rel_bias_table[:, :NSA_H]
    tab_diff = rel_bias_table[:, NSA_H:]
    Bp, S, _ = x_prompt.shape
    Bs, T, _ = x_sample.shape
    past = page_table.shape[1] * cache_fox_kv.shape[2]
    win_buf = state_nsa_win_kv.shape[2]
    L = past + T
    pos_q_s = past + jnp.arange(T)
    pos_k_p = jnp.arange(S)
    pos_k_s = jnp.arange(L)
    pos_w_p = jnp.arange(S + WINDOW) - WINDOW
    pos_w_s = past - win_buf + jnp.arange(win_buf + T)
    xp, xs = x_prompt, x_sample
    st = {n: [] for n in ('cmp_p', 'cmp_s', 'sel_p', 'sel_s', 'win_p', 'win_s', 'fkv_p', 'fkv_s',
                          'flf_p', 'flf_s', 'dkv_p', 'dkv_s')}
    for l in range(DEPTH):
        lam_init = 0.8 - 0.6 * math.exp(-0.3 * l)
        lp = diff_lambda[l].astype(jnp.float32)
        lam = jnp.exp(jnp.sum(lp[0] * lp[1])) - jnp.exp(jnp.sum(lp[2] * lp[3])) + lam_init

        nq, ng, cmp_r, sel_r, win_r, fq, fox_r, logf, dq, diff_r, mg = project(
            rms_norm(xp, norm_mix[l]), w_in[l], b_forget[l], qk_gain[l])
        kc, vc, ksb, vsb = nsa_prepare(cmp_r, sel_r, nsa_cmp_w[l], nsa_cmp_pe[l])
        kwp = jnp.pad(win_r, ((0, 0), (WINDOW, 0), (0, 0), (0, 0), (0, 0)))
        c_p = jnp.cumsum(logf, axis=1).reshape(Bp, S, FOX_KVH, FOX_H // FOX_KVH)

        def nsa_blk(i):
            t0 = i * Q_BLOCK
            wb = dslice(kwp, t0, Q_BLOCK + WINDOW)
            pw = lax.dynamic_slice_in_dim(pos_w_p, t0, Q_BLOCK + WINDOW, axis=0)
            return nsa_block(dslice(nq, t0, Q_BLOCK), t0 + jnp.arange(Q_BLOCK), kc, vc, ksb, vsb,
                             wb[:, :, 0], wb[:, :, 1], pw, dslice(ng, t0, Q_BLOCK), tab_nsa)

        def fox_blk(i):
            t0 = i * Q_BLOCK
            return fox_block(dslice(fq, t0, Q_BLOCK), dslice(c_p, t0, Q_BLOCK), t0 + jnp.arange(Q_BLOCK),
                             fox_r[:, :, 0], fox_r[:, :, 1], c_p, pos_k_p)

        def diff_blk(i):
            t0 = i * Q_BLOCK
            return diff_block(dslice(dq, t0, Q_BLOCK), t0 + jnp.arange(Q_BLOCK), diff_r, pos_k_p, lam, tab_diff)

        o_a = sweep(nsa_blk, S, Bp)
        o_b = sweep(fox_blk, S, Bp)
        o_c = diff_post(sweep(diff_blk, S, Bp).reshape(Bp, S, DIFF_KVH, DIFF_H // DIFF_KVH, 2 * HEAD_DIM),
                        diff_subln[l], lam_init)
        xp = xp + merge(o_a, o_b, o_c, mg, w_branch[l], w_out[l])
        xp = xp + swiglu(rms_norm(xp, norm_ffn[l]), w_gate_up[l], w_down[l])
        st['cmp_p'].append(cmp_r)
        st['sel_p'].append(sel_r)
        st['win_p'].append(win_r[:, S - min(WINDOW, S):])
        st['fkv_p'].append(fox_r)
        st['flf_p'].append(logf)
        st['dkv_p'].append(diff_r)

        nq, ng, cmp_r, sel_r, win_r, fq, fox_r, logf, dq, diff_r, mg = project(
            rms_norm(xs, norm_mix[l]), w_in[l], b_forget[l], qk_gain[l])
        cmp_full = jnp.concatenate([gather_pages(cache_nsa_cmp_kv[l], page_table), cmp_r], axis=1)
        sel_full = jnp.concatenate([gather_pages(cache_nsa_sel_kv[l], page_table), sel_r], axis=1)
        win_full = jnp.concatenate([state_nsa_win_kv[l], win_r], axis=1)
        kc, vc, ksb, vsb = nsa_prepare(cmp_full, sel_full, nsa_cmp_w[l], nsa_cmp_pe[l])
        o_a = nsa_block(nq, pos_q_s, kc, vc, ksb, vsb, win_full[:, :, 0], win_full[:, :, 1], pos_w_s, ng,
                        tab_nsa).reshape(Bs, T, BRANCH_W)
        fox_full = jnp.concatenate([gather_pages(cache_fox_kv[l], page_table), fox_r], axis=1)
        lf_full = jnp.concatenate([gather_pages(cache_fox_logf[l], page_table).astype(jnp.float32), logf], axis=1)
        c_s = jnp.cumsum(lf_full, axis=1).reshape(Bs, L, FOX_KVH, FOX_H // FOX_KVH)
        o_b = fox_block(fq, c_s[:, past:], pos_q_s, fox_full[:, :, 0], fox_full[:, :, 1], c_s,
                        pos_k_s).reshape(Bs, T, BRANCH_W)
        diff_full = jnp.concatenate([gather_pages(cache_diff_kv[l], page_table), diff_r], axis=1)
        o_c = diff_post(diff_block(dq, pos_q_s, diff_full, pos_k_s, lam, tab_diff), diff_subln[l], lam_init)
        xs = xs + merge(o_a, o_b, o_c, mg, w_branch[l], w_out[l])
        xs = xs + swiglu(rms_norm(xs, norm_ffn[l]), w_gate_up[l], w_down[l])
        st['cmp_s'].append(cmp_r)
        st['sel_s'].append(sel_r)
        st['win_s'].append(win_full[:, win_full.shape[1] - win_buf:])
        st['fkv_s'].append(fox_r)
        st['flf_s'].append(logf)
        st['dkv_s'].append(diff_r)

    ns = {n: jnp.stack(v, axis=0) for n, v in st.items()}
    return (xp, xs, ns['cmp_p'], ns['cmp_s'], ns['sel_p'], ns['sel_s'], ns['win_p'], ns['win_s'],
            ns['fkv_p'], ns['fkv_s'], ns['flf_p'], ns['flf_s'], ns['dkv_p'], ns['dkv_s'])
```

```python
import functools
import math

import numpy as np
import jax
import jax.numpy as jnp
from jax import lax
from jax.experimental import pallas as pl
from jax.experimental.pallas import tpu as pltpu

F32 = jnp.float32
BF16 = jnp.bfloat16

HEAD_DIM = 64
NSA_KVH = 2
FOX_KVH = 4
DIFF_KVH = 2
N_BRANCH = 3
L_CMP = 32
D_CMP = 16
L_SEL = 64
N_SEL = 16
WINDOW = 512
N_BUCKETS = 32
MAX_DISTANCE = 128
FORGET_GROUPS = 8
RMS_EPS = 1e-6
NEG_BIG = -1e30
MASKED_BELOW = -1e29

LANES = 128
SUBLANES = 8
MXU_DIM = 256
VMEM_LIMIT = 56 * 1024 * 1024

ROW_TILE = 256
NSA_Q_TILE = 128
FLASH_TILE = 256
CUMSUM_TILE = 512
PAGE = 128


def _cparams(sem):
    return pltpu.CompilerParams(dimension_semantics=sem, vmem_limit_bytes=VMEM_LIMIT)


def _resident(shape):
    zeros = (0,) * len(shape)
    return pl.BlockSpec(shape, lambda *_: zeros)


def _dot(a, b):
    return jnp.dot(a, b, preferred_element_type=F32)


def _dot_nt(a, b):
    return lax.dot_general(a, b, (((1,), (1,)), ((), ())), preferred_element_type=F32)


def _split3(x):
    hi = x.astype(BF16)
    r = x - hi.astype(F32)
    mid = r.astype(BF16)
    lo = (r - mid.astype(F32)).astype(BF16)
    return hi, mid, lo


def _split2(x):
    hi = x.astype(BF16)
    return hi, (x - hi.astype(F32)).astype(BF16)


def _split3_trunc(x):
    def top(v):
        bits = lax.bitcast_convert_type(v, jnp.uint32) & jnp.uint32(0xFFFF0000)
        return lax.bitcast_convert_type(bits, F32)
    hi = top(x)
    mid = top(x - hi)
    lo = top(x - hi - mid)
    return hi.astype(BF16), mid.astype(BF16), lo.astype(BF16)


def _masked_softmax(s, valid):
    s = jnp.where(valid, s, NEG_BIG)
    m = jnp.max(s, axis=-1, keepdims=True)
    p = jnp.exp(s - m) * valid.astype(F32)
    return p / jnp.maximum(jnp.sum(p, axis=-1, keepdims=True), 1e-30)


def _t5_thresholds():
    n = np.arange(0, 4 * MAX_DISTANCE)
    exact = N_BUCKETS // 2
    nf = np.maximum(n, 1).astype(np.float32)
    large = exact + (np.log(nf / np.float32(exact)) / np.float32(math.log(MAX_DISTANCE / exact))
                     * np.float32(N_BUCKETS - exact)).astype(np.int32)
    b = np.where(n < exact, n, np.minimum(large, N_BUCKETS - 1))
    assert np.all(np.diff(b) >= 0) and b[-1] == N_BUCKETS - 1
    return tuple(int(np.argmax(b >= k)) for k in range(1, N_BUCKETS))


_T5_THR = _t5_thresholds()


def _t5_kernel(dist_ref, tab_ref, out_ref):
    d = dist_ref[...]
    n = jnp.maximum(d, 0)
    tab = tab_ref[...]
    val = jnp.broadcast_to(tab[:, 0:1], d.shape)
    for k in range(1, N_BUCKETS):
        val = jnp.where(n >= _T5_THR[k - 1], tab[:, k:k + 1], val)
    out_ref[...] = jnp.where(d < 0, NEG_BIG, val - tab[:, N_BUCKETS - 1:N_BUCKETS])


def _t5_bias(dist, head_of_row, table):
    rows, cols = dist.shape
    blk = math.gcd(rows, 256)
    tab_rows = jnp.transpose(table)[jnp.asarray(head_of_row, jnp.int32)]
    return pl.pallas_call(
        _t5_kernel,
        grid=(rows // blk,),
        in_specs=[pl.BlockSpec((blk, cols), lambda i: (i, 0)),
                  pl.BlockSpec((blk, N_BUCKETS), lambda i: (i, 0))],
        out_specs=pl.BlockSpec((blk, cols), lambda i: (i, 0)),
        out_shape=jax.ShapeDtypeStruct((rows, cols), F32),
        compiler_params=_cparams(("arbitrary",)),
        name="t5_bias",
    )(jnp.asarray(dist, jnp.int32), tab_rows.astype(F32))


def _proj_layout(d_model):
    nsa_h = (d_model // 2) // HEAD_DIM
    splits = (nsa_h * HEAD_DIM, 6 * NSA_KVH * HEAD_DIM, 3 * nsa_h,
              nsa_h * HEAD_DIM, 2 * FOX_KVH * HEAD_DIM, nsa_h,
              (nsa_h // 2) * 2 * HEAD_DIM, 2 * DIFF_KVH * 2 * HEAD_DIM, N_BRANCH * d_model)
    off = np.concatenate([[0], np.cumsum(splits)])
    nq, nkv, ng, fq, fkv, ff, dq, dkv, mg = [np.arange(off[i], off[i + 1]) for i in range(9)]
    small_pad = LANES - len(ng) - len(ff)
    cols = np.concatenate([nq, nkv, fq, fkv, dq, dkv, mg, ng, ff])
    return cols, small_pad, int(off[-1])


def _proj_kernel(x_ref, g_ref, w_ref, gain_ref, bf_ref, m_ref,
                 nq_ref, cmp_ref, sel_ref, win_ref, fq_ref, fox_ref, dq_ref, diff_ref, mg_ref, small_ref,
                 *, d_model):
    bw = d_model // 2
    x = x_ref[...]
    xn = x * lax.rsqrt(jnp.mean(x * x, axis=-1, keepdims=True) + RMS_EPS) * g_ref[...]
    xb = xn.astype(BF16)

    def seg(a, b):
        return _dot(xb, w_ref[:, a:b])

    def head_norm(p, a):
        outs = []
        wdt = p.shape[1]
        step = min(wdt, MXU_DIM)
        for c in range(0, wdt, step):
            pc = p[:, c:c + step]
            ms = _dot((pc * pc).astype(BF16), m_ref[:step, :step]) * (1.0 / HEAD_DIM)
            outs.append(pc * lax.rsqrt(ms + RMS_EPS) * gain_ref[:, a + c:a + c + step])
        return outs[0] if len(outs) == 1 else jnp.concatenate(outs, axis=1)

    scale = HEAD_DIM ** -0.5
    o = 0
    nq_ref[...] = (head_norm(seg(o, o + bw), o) * scale).astype(BF16)
    o += bw
    kvw = NSA_KVH * HEAD_DIM
    for ref in (cmp_ref, sel_ref, win_ref):
        p = seg(o, o + 2 * kvw)
        ref[...] = jnp.concatenate([head_norm(p[:, :kvw], o), p[:, kvw:]], axis=1)
        o += 2 * kvw
    fq_ref[...] = (head_norm(seg(o, o + bw), o) * scale).astype(BF16)
    o += bw
    fkw = FOX_KVH * HEAD_DIM
    p = seg(o, o + 2 * fkw)
    fox_ref[...] = jnp.concatenate([head_norm(p[:, :fkw], o), p[:, fkw:]], axis=1)
    o += 2 * fkw
    dq_ref[...] = (head_norm(seg(o, o + bw), o) * scale).astype(BF16)
    o += bw
    dkw = DIFF_KVH * 2 * HEAD_DIM
    p = seg(o, o + 2 * dkw)
    diff_ref[...] = jnp.concatenate([head_norm(p[:, :dkw], o), p[:, dkw:]], axis=1)
    o += 2 * dkw
    chunk = 2 * MXU_DIM
    for c in range(0, N_BRANCH * d_model, chunk):
        mg_ref[:, c:c + chunk] = jax.nn.sigmoid(seg(o + c, o + c + chunk))
    o += N_BRANCH * d_model
    p = seg(o, o + LANES)
    n_gate = 3 * (bw // HEAD_DIM)
    lane = lax.broadcasted_iota(jnp.int32, p.shape, 1)
    z = p + bf_ref[...]
    logf = jnp.minimum(z, 0.0) - jnp.log1p(jnp.exp(-jnp.abs(z)))
    small_ref[...] = jnp.where(lane < n_gate, jax.nn.sigmoid(p),
                               jnp.where(lane < n_gate + FORGET_GROUPS, logf, 0.0))


def _project(x2d, g_norm, w_perm, gain_row, bf_row, m_blk):
    n, d_model = x2d.shape
    bw = d_model // 2
    wtot = w_perm.shape[1]
    tm = min(ROW_TILE, n)
    kvw = 2 * NSA_KVH * HEAD_DIM
    widths = [(bw, BF16), (kvw, F32), (kvw, F32), (kvw, F32), (bw, BF16), (2 * FOX_KVH * HEAD_DIM, F32),
              (bw, BF16), (2 * DIFF_KVH * 2 * HEAD_DIM, F32), (N_BRANCH * d_model, F32), (LANES, F32)]
    row = lambda w: pl.BlockSpec((tm, w), lambda i: (i, 0))
    return pl.pallas_call(
        functools.partial(_proj_kernel, d_model=d_model),
        grid=(n // tm,),
        in_specs=[row(d_model), _resident((1, d_model)), _resident((d_model, wtot)), _resident((1, wtot)),
                  _resident((1, LANES)), _resident((MXU_DIM, MXU_DIM))],
        out_specs=[row(w) for w, _ in widths],
        out_shape=[jax.ShapeDtypeStruct((n, w), dt) for w, dt in widths],
        compiler_params=_cparams(("arbitrary",)),
        name="proj",
    )(x2d, g_norm, w_perm, gain_row, bf_row, m_blk)


def _cumsum_kernel(x_ref, tri_ref, out_ref, carry_ref):
    @pl.when(pl.program_id(1) == 0)
    def _():
        carry_ref[...] = jnp.zeros_like(carry_ref)

    tri = tri_ref[...]
    hi, mid, lo = _split3(x_ref[0])
    cs = _dot(tri, hi) + _dot(tri, mid) + _dot(tri, lo) + carry_ref[...]
    out_ref[0] = cs
    carry_ref[...] = cs[cs.shape[0] - 1:, :]


def _cumsum_rows(x):
    b, s, w = x.shape
    t = min(CUMSUM_TILE, s)
    tri = jnp.asarray(np.tril(np.ones((t, t), np.float32)), BF16)
    return pl.pallas_call(
        _cumsum_kernel,
        grid=(b, s // t),
        in_specs=[pl.BlockSpec((1, t, w), lambda i, j: (i, j, 0)), _resident((t, t))],
        out_specs=pl.BlockSpec((1, t, w), lambda i, j: (i, j, 0)),
        out_shape=jax.ShapeDtypeStruct((b, s, w), F32),
        scratch_shapes=[pltpu.VMEM((1, w), F32)],
        compiler_params=_cparams(("arbitrary", "arbitrary")),
        name="cumsum",
    )(x, tri)


def _compress_tokens(x, pe1, pe2, w1, w2):
    first = _dot((x + pe1).astype(BF16), w1)
    second = _dot((x + pe2).astype(BF16), w2)
    n = x.shape[0]
    return first + pltpu.roll(second, shift=n - 1, axis=0)


def _compress_prompt_kernel(x_ref, pe1_ref, pe2_ref, w1_ref, w2_ref, out_ref, *, front):
    tok = _compress_tokens(x_ref[0], pe1_ref[...], pe2_ref[...], w1_ref[...], w2_ref[...])
    n, w = tok.shape
    out_ref[0, 0:front, :] = jnp.zeros((front, w), F32)
    out_ref[0, front:front + n, :] = tok
    out_ref[0, front + n:, :] = jnp.zeros((SUBLANES, w), F32)


def _compress_prompt(x, pe1, pe2, w1, w2):
    b, n, cw = x.shape
    w = w1.shape[1]
    front = n - SUBLANES
    return pl.pallas_call(
        functools.partial(_compress_prompt_kernel, front=front),
        grid=(b,),
        in_specs=[pl.BlockSpec((1, n, cw), lambda i: (i, 0, 0)), _resident((1, cw)), _resident((1, cw)),
                  _resident((cw, w)), _resident((cw, w))],
        out_specs=pl.BlockSpec((1, 2 * n, w), lambda i: (i, 0, 0)),
        out_shape=jax.ShapeDtypeStruct((b, 2 * n, w), F32),
        compiler_params=_cparams(("arbitrary",)),
        name="compress_prompt",
    )(x, pe1, pe2, w1, w2)


def _compress_sample_kernel(pt_ref, *refs, n_pages):
    del pt_ref
    page_refs = refs[:n_pages]
    new_ref, pe1_ref, pe2_ref, w1_ref, w2_ref, out_ref = refs[n_pages:]
    x = jnp.concatenate([r[0, 0] for r in page_refs] + [new_ref[0]], axis=0)
    tok = _compress_tokens(x, pe1_ref[...], pe2_ref[...], w1_ref[...], w2_ref[...])
    out_ref[0] = tok[:out_ref.shape[1], :]


def _compress_sample(pool, layer, page_table, new_chunk, pe1, pe2, w1, w2):
    bs, n_pages = page_table.shape
    rpp, cw = pool.shape[2], pool.shape[3]
    w = w1.shape[1]
    n_tok = n_pages * rpp
    page_specs = [pl.BlockSpec((1, 1, rpp, cw), lambda b, pt, j=j: (layer, pt[b, j], 0, 0)) for j in range(n_pages)]
    const = lambda shape: pl.BlockSpec(shape, lambda b, pt: (0,) * len(shape))
    return pl.pallas_call(
        functools.partial(_compress_sample_kernel, n_pages=n_pages),
        grid_spec=pltpu.PrefetchScalarGridSpec(
            num_scalar_prefetch=1, grid=(bs,),
            in_specs=page_specs + [pl.BlockSpec((1, SUBLANES, cw), lambda b, pt: (b, 0, 0)),
                                   const((1, cw)), const((1, cw)), const((cw, w)), const((cw, w))],
            out_specs=pl.BlockSpec((1, n_tok, w), lambda b, pt: (b, 0, 0))),
        out_shape=jax.ShapeDtypeStruct((bs, n_tok, w), F32),
        compiler_params=_cparams(("arbitrary",)),
        name="compress_sample",
    )(page_table, *([pool] * n_pages), new_chunk, pe1, pe2, w1, w2)


def _select_top_blocks(score_t, n_top):
    nb = score_t.shape[0]
    n_f = lax.broadcasted_iota(jnp.int32, score_t.shape, 0).astype(F32)
    taken = jnp.zeros(score_t.shape, F32)
    for _ in range(n_top):
        free = taken < 0.5
        cand = jnp.where(free, score_t, -jnp.inf)
        m = jnp.max(cand, axis=0, keepdims=True)
        idx = jnp.min(jnp.where(free & (cand == m), n_f, float(nb)), axis=0, keepdims=True)
        taken = jnp.where(n_f == idx, 1.0, taken)
    return taken


def _nsa_local_kernel(q_ref, tok_ref, cb_ref, win_ref, wb_ref, small_ref, out_ref, sel_ref, *, n_chunk, n_top):
    tq = NSA_Q_TILE
    i = pl.program_id(1)
    t0 = i * tq
    n_heads = cb_ref.shape[0]
    rep = n_heads // NSA_KVH
    q = q_ref[0]
    gates = small_ref[0]
    tok = tok_ref[0, pl.ds(pl.multiple_of(i * (tq // D_CMP), SUBLANES), n_chunk), :]
    wrows = win_ref[0, pl.ds(pl.multiple_of(t0, tq), tq + WINDOW), :]
    kvw = NSA_KVH * HEAD_DIM
    front = n_chunk - SUBLANES
    w_io = lax.broadcasted_iota(jnp.int32, (tq, n_chunk), 1)
    c_exists = w_io >= front - i * (tq // D_CMP)
    lw_io = lax.broadcasted_iota(jnp.int32, (tq, tq + WINDOW), 1)
    w_exists = lw_io >= WINDOW - t0
    n_io = lax.broadcasted_iota(jnp.int32, (LANES, n_chunk), 0)
    c_io = lax.broadcasted_iota(jnp.int32, (LANES, n_chunk), 1) + (i * (tq // D_CMP) - front)
    ov_t = jnp.where((c_io * D_CMP < (n_io + 1) * L_SEL) & (c_io * D_CMP + L_CMP > n_io * L_SEL), 1.0, 0.0).astype(BF16)
    blk = lax.broadcasted_iota(jnp.int32, (LANES, tq), 0)
    cur = 2 * i + (lax.broadcasted_iota(jnp.int32, (LANES, tq), 1) >= L_SEL).astype(jnp.int32)
    for g in range(NSA_KVH):
        kc = tok[:, g * HEAD_DIM:(g + 1) * HEAD_DIM].astype(BF16)
        vc = tok[:, kvw + g * HEAD_DIM:kvw + (g + 1) * HEAD_DIM].astype(BF16)
        kw = wrows[:, g * HEAD_DIM:(g + 1) * HEAD_DIM]
        vw = wrows[:, kvw + g * HEAD_DIM:kvw + (g + 1) * HEAD_DIM]
        psum = jnp.zeros((tq, n_chunk), F32)
        for r in range(rep):
            h = g * rep + r
            qh = q[:, h * HEAD_DIM:(h + 1) * HEAD_DIM]
            cb = cb_ref[h]
            p_c = _masked_softmax(_dot_nt(qh, kc) + cb, (cb > MASKED_BELOW) & c_exists)
            o_c = _dot(p_c.astype(BF16), vc)
            psum = psum + p_c
            wb = wb_ref[h]
            p_w = _masked_softmax(_dot_nt(qh, kw) + wb, (wb > MASKED_BELOW) & w_exists)
            o_w = _dot(p_w.astype(BF16), vw)
            out_ref[0, :, h * HEAD_DIM:(h + 1) * HEAD_DIM] = (gates[:, 3 * h:3 * h + 1] * o_c
                                                             + gates[:, 3 * h + 2:3 * h + 3] * o_w)
        p_hi, p_lo = _split2(psum)
        imp_t = _dot_nt(ov_t, p_hi) + _dot_nt(ov_t, p_lo)
        forced = (blk == 0) | (blk == cur) | (blk == cur - 1)
        score = jnp.where(blk > cur, -jnp.inf, jnp.where(forced, jnp.inf, imp_t))
        taken = _select_top_blocks(score, n_top)
        sel_ref[0, g] = jnp.where(jnp.transpose(taken) > 0.5, 0.0, NEG_BIG).astype(BF16)


def _nsa_local(q, tok, cb, win, wb, small):
    b, s, qw = q.shape
    n_chunk = tok.shape[1] // 2
    n_heads = cb.shape[0]
    tq = NSA_Q_TILE
    n_top = min(N_SEL, s // L_SEL)
    return pl.pallas_call(
        functools.partial(_nsa_local_kernel, n_chunk=n_chunk, n_top=n_top),
        grid=(b, s // tq),
        in_specs=[pl.BlockSpec((1, tq, qw), lambda bi, i: (bi, i, 0)),
                  pl.BlockSpec((1,) + tok.shape[1:], lambda bi, i: (bi, 0, 0)),
                  _resident(cb.shape),
                  pl.BlockSpec((1,) + win.shape[1:], lambda bi, i: (bi, 0, 0)),
                  _resident(wb.shape),
                  pl.BlockSpec((1, tq, LANES), lambda bi, i: (bi, i, 0))],
        out_specs=[pl.BlockSpec((1, tq, qw), lambda bi, i: (bi, i, 0)),
                   pl.BlockSpec((1, NSA_KVH, tq, LANES), lambda bi, i: (bi, 0, i, 0))],
        out_shape=[jax.ShapeDtypeStruct((b, s, qw), F32),
                   jax.ShapeDtypeStruct((b, NSA_KVH, s, LANES), BF16)],
        compiler_params=_cparams(("arbitrary", "arbitrary")),
        name="nsa_local",
    )(q, tok, cb, win, wb, small)


def _flash_kernel(*refs, kind, n_units, dv, n_near, lam_init, gate_base):
    if kind == "diff":
        q_ref, k_ref, v_ref, near_ref, lam_ref, subln_ref, out_ref, m_ref, acc_ref = refs
    elif kind == "sel":
        q_ref, k_ref, v_ref, near_ref, small_ref, out_ref, m_ref, acc_ref = refs
    else:
        q_ref, k_ref, v_ref, near_ref, out_ref, m_ref, acc_ref = refs
    t = FLASH_TILE
    g = pl.program_id(1)
    i = pl.program_id(2)
    m_ref[...] = jnp.full(m_ref.shape, NEG_BIG, F32)
    acc_ref[...] = jnp.zeros(acc_ref.shape, F32)

    def step(j, near_idx):
        start = pl.multiple_of(j * t, t)
        kt = k_ref[0, 0, pl.ds(start, t), :]
        vt = v_ref[0, 0, pl.ds(start, t), :]
        for u in range(n_units):
            s = _dot_nt(q_ref[0, 0, u], kt)
            if near_idx is not None:
                s = s + near_ref[0, u if near_ref.shape[1] > 1 else 0, near_idx]
            m_old = m_ref[u]
            m_new = jnp.maximum(m_old, jnp.max(s, axis=-1, keepdims=True))
            p = jnp.exp(s - m_new)
            acc_ref[u] = acc_ref[u] * jnp.exp(m_old - m_new) + _dot(p.astype(BF16), vt)
            m_ref[u] = m_new

    def far_body(j, carry):
        step(j, None)
        return carry

    lax.fori_loop(0, jnp.maximum(i - (n_near - 1), 0), far_body, 0)
    if n_near == 2:
        @pl.when(i >= 1)
        def _():
            step(i - 1, 1)
    step(i, 0)

    def unit_out(u):
        acc = acc_ref[u]
        return acc[:, :dv] / acc[:, dv:dv + 1]

    if kind == "diff":
        lp = lam_ref[...]
        lam = (jnp.exp(jnp.sum(lp[0:1] * lp[1:2], axis=-1, keepdims=True))
               - jnp.exp(jnp.sum(lp[2:3] * lp[3:4], axis=-1, keepdims=True)) + lam_init)
        for r in range(n_units // 2):
            o = unit_out(2 * r) - lam * unit_out(2 * r + 1)
            y = o * lax.rsqrt(jnp.mean(o * o, axis=-1, keepdims=True) + RMS_EPS) * subln_ref[...]
            out_ref[0, :, r * dv:(r + 1) * dv] = y * (1.0 - lam_init)
    elif kind == "sel":
        gates = small_ref[0]
        for u in range(n_units):
            col = 3 * (g * n_units + u) + gate_base
            lane = lax.broadcasted_iota(jnp.int32, gates.shape, 1)
            gate = jnp.sum(jnp.where(lane == col, gates, 0.0), axis=-1, keepdims=True)
            out_ref[0, :, u * dv:(u + 1) * dv] = gate * unit_out(u)
    else:
        for u in range(n_units):
            out_ref[0, :, u * dv:(u + 1) * dv] = unit_out(u)


def _flash(kind, q, k, v, near, extras, *, dv, out_width, lam_init=0.0):
    b, n_g, n_units, s, dk = q.shape
    dva = v.shape[-1]
    t = FLASH_TILE
    n_near = near.shape[2]
    wg = out_width // n_g
    in_specs = [pl.BlockSpec((1, 1, n_units, t, dk), lambda bi, g, i: (bi, g, 0, i, 0)),
                pl.BlockSpec((1, 1, s, dk), lambda bi, g, i: (bi, g, 0, 0)),
                pl.BlockSpec((1, 1, s, dva), lambda bi, g, i: (bi, g, 0, 0))]
    if near.shape[0] > 1:
        in_specs.append(pl.BlockSpec((1,) + near.shape[1:], lambda bi, g, i: (g, 0, 0, 0, 0)))
    else:
        in_specs.append(_resident(near.shape))
    if kind == "diff":
        in_specs += [_resident(extras[0].shape), _resident(extras[1].shape)]
    elif kind == "sel":
        in_specs.append(pl.BlockSpec((1, t, LANES), lambda bi, g, i: (bi, i, 0)))
    return pl.pallas_call(
        functools.partial(_flash_kernel, kind=kind, n_units=n_units, dv=dv, n_near=n_near, lam_init=lam_init,
                          gate_base=1),
        grid=(b, n_g, s // t),
        in_specs=in_specs,
        out_specs=pl.BlockSpec((1, t, wg), lambda bi, g, i: (bi, i, g)),
        out_shape=jax.ShapeDtypeStruct((b, s, out_width), F32),
        scratch_shapes=[pltpu.VMEM((n_units, t, 1), F32), pltpu.VMEM((n_units, t, dva), F32)],
        compiler_params=_cparams(("arbitrary", "arbitrary", "arbitrary")),
        name="flash_" + kind,
    )(q, k, v, near, *extras)


def _merge_kernel(*refs, n_nsa):
    nsa_refs = refs[:n_nsa]
    fox_ref, diff_ref, mg_ref, x_ref, wbr_ref, wo_ref, out_ref = refs[n_nsa:]
    d = x_ref.shape[1]
    o_a = nsa_refs[0][...]
    for r in nsa_refs[1:]:
        o_a = o_a + r[...]
    mix = jnp.zeros(x_ref.shape, F32)
    for n, o in enumerate((o_a, fox_ref[...], diff_ref[...])):
        mix = mix + mg_ref[:, n * d:(n + 1) * d] * _dot(o.astype(BF16), wbr_ref[n])
    out_ref[...] = x_ref[...] + _dot(mix.astype(BF16), wo_ref[...])


def _merge(nsa_parts, o_fox, o_diff, mg, x2d, w_br, w_o):
    n, d = x2d.shape
    bw = o_fox.shape[1]
    tm = min(ROW_TILE, n)
    row = lambda w: pl.BlockSpec((tm, w), lambda i: (i, 0))
    return pl.pallas_call(
        functools.partial(_merge_kernel, n_nsa=len(nsa_parts)),
        grid=(n // tm,),
        in_specs=[row(bw)] * (len(nsa_parts) + 2) + [row(N_BRANCH * d), row(d), _resident(w_br.shape),
                                                     _resident(w_o.shape)],
        out_specs=row(d),
        out_shape=jax.ShapeDtypeStruct((n, d), F32),
        compiler_params=_cparams(("arbitrary",)),
        name="merge",
    )(*nsa_parts, o_fox, o_diff, mg, x2d, w_br, w_o)


def _ffn_kernel(x_ref, g_ref, wgu_ref, wd_ref, out_ref, *, d_ff):
    x = x_ref[...]
    h = (x * lax.rsqrt(jnp.mean(x * x, axis=-1, keepdims=True) + RMS_EPS) * g_ref[...]).astype(BF16)
    acc = x
    for c in range(0, d_ff, MXU_DIM):
        gate = _dot(h, wgu_ref[:, c:c + MXU_DIM])
        up = _dot(h, wgu_ref[:, d_ff + c:d_ff + c + MXU_DIM])
        acc = acc + _dot((jax.nn.silu(gate) * up).astype(BF16), wd_ref[c:c + MXU_DIM, :])
    out_ref[...] = acc


def _ffn(x2d, g_norm, w_gu, w_d):
    n, d = x2d.shape
    d_ff = w_d.shape[0]
    tm = min(ROW_TILE, n)
    row = pl.BlockSpec((tm, d), lambda i: (i, 0))
    return pl.pallas_call(
        functools.partial(_ffn_kernel, d_ff=d_ff),
        grid=(n // tm,),
        in_specs=[row, _resident((1, d)), _resident(w_gu.shape), _resident(w_d.shape)],
        out_specs=row,
        out_shape=jax.ShapeDtypeStruct((n, d), F32),
        compiler_params=_cparams(("arbitrary",)),
        name="ffn",
    )(x2d, g_norm, w_gu, w_d)


def _fold_groups(o, group_of_row, n_groups, width):
    out = jnp.zeros((o.shape[0], width), F32)
    for gi in range(n_groups):
        out = out + jnp.where(group_of_row == gi, o[:, gi * width:(gi + 1) * width], 0.0)
    return out


def _row_group(rows, period, per_group):
    r = lax.broadcasted_iota(jnp.int32, (rows, 1), 0)
    return (r % period) // per_group


def _samp_cmp_kernel(q_ref, tok_ref, cb_ref, ov_ref, rsum_ref, cur_ref, expand_ref, t5_ref, gate_ref,
                     out_ref, bias_ref, *, n_top):
    q = q_ref[0]
    tok = tok_ref[0]
    kvw = NSA_KVH * HEAD_DIM
    cb = cb_ref[...]
    p = _masked_softmax(_dot_nt(q, tok[:, :kvw].astype(BF16)) + cb, cb > MASKED_BELOW)
    o = _dot(p.astype(BF16), tok[:, kvw:].astype(BF16))
    rows = q.shape[0]
    rep = rows // (SUBLANES * NSA_KVH)
    grp = _row_group(rows, NSA_KVH * rep, rep)
    out_ref[0] = gate_ref[0][:, 0:1] * _fold_groups(o, grp, NSA_KVH, HEAD_DIM)
    p_hi, p_lo = _split2(p)
    psum = _dot(rsum_ref[...], p_hi) + _dot(rsum_ref[...], p_lo)
    s_hi, s_lo = _split2(psum)
    imp = _dot(s_hi, ov_ref[...]) + _dot(s_lo, ov_ref[...])
    blk = lax.broadcasted_iota(jnp.int32, imp.shape, 1)
    cur = cur_ref[...]
    forced = (blk == 0) | (blk == cur) | (blk == cur - 1)
    score = jnp.where(blk > cur, -jnp.inf, jnp.where(forced, jnp.inf, imp))
    n_f = blk.astype(F32)
    taken = jnp.zeros(imp.shape, F32)
    for _ in range(n_top):
        free = taken < 0.5
        cand = jnp.where(free, score, -jnp.inf)
        m = jnp.max(cand, axis=-1, keepdims=True)
        idx = jnp.min(jnp.where(free & (cand == m), n_f, float(imp.shape[1])), axis=-1, keepdims=True)
        taken = jnp.where(n_f == idx, 1.0, taken)
    key_sel = _dot(taken.astype(BF16), expand_ref[...])
    bias_ref[0] = jnp.where(key_sel > 0.5, t5_ref[...], NEG_BIG)


def _samp_cmp(qbd, tok, cb, ov, rsum, cur, expand, t5, gate, n_top):
    bs, rows, dk = qbd.shape
    lp = expand.shape[1]
    per_b = lambda shape: pl.BlockSpec((1,) + shape[1:], lambda b: (b,) + (0,) * (len(shape) - 1))
    return pl.pallas_call(
        functools.partial(_samp_cmp_kernel, n_top=n_top),
        grid=(bs,),
        in_specs=[per_b(qbd.shape), per_b(tok.shape), _resident(cb.shape), _resident(ov.shape), _resident(rsum.shape),
                  _resident(cur.shape), _resident(expand.shape), _resident(t5.shape), per_b(gate.shape)],
        out_specs=[pl.BlockSpec((1, rows, HEAD_DIM), lambda b: (b, 0, 0)),
                   pl.BlockSpec((1, rows, lp), lambda b: (b, 0, 0))],
        out_shape=[jax.ShapeDtypeStruct((bs, rows, HEAD_DIM), F32), jax.ShapeDtypeStruct((bs, rows, lp), F32)],
        compiler_params=_cparams(("arbitrary",)),
        name="samp_cmp",
    )(qbd, tok, cb, ov, rsum, cur, expand, t5, gate)


def _samp_fox_bias_kernel(pt_ref, *refs, n_pages, n_new):
    del pt_ref
    page_refs = refs[:n_pages]
    new_ref, tri_ref, out_ref = refs[n_pages:]
    tri = tri_ref[...]

    def lane_cumsum(x):
        hi, mid, lo = _split3(x)
        return _dot(hi, tri) + _dot(mid, tri) + _dot(lo, tri)

    carry = jnp.zeros((FORGET_GROUPS, 1), F32)
    c_tiles = []
    for r in page_refs:
        c = lane_cumsum(r[0, 0]) + carry
        carry = c[:, PAGE - 1:PAGE]
        c_tiles.append(c)
    c_new = lane_cumsum(new_ref[0]) + carry
    lane = lax.broadcasted_iota(jnp.int32, (FORGET_GROUPS, PAGE), 1)
    for t in range(n_new):
        c_q = c_new[:, t:t + 1]
        lo_r, hi_r = t * FORGET_GROUPS, (t + 1) * FORGET_GROUPS
        for j, c in enumerate(c_tiles):
            out_ref[0, lo_r:hi_r, j * PAGE:(j + 1) * PAGE] = c_q - c
        out_ref[0, lo_r:hi_r, n_pages * PAGE:] = jnp.where(lane <= t, c_q - c_new, NEG_BIG)


def _samp_fox_bias(pool_t, layer, page_table, logf_new_t):
    bs, n_pages = page_table.shape
    n_new = SUBLANES
    lp = (n_pages + 1) * PAGE
    tri = jnp.asarray(np.triu(np.ones((PAGE, PAGE), np.float32)), BF16)
    page_specs = [pl.BlockSpec((1, 1, FORGET_GROUPS, PAGE), lambda b, pt, j=j: (layer, pt[b, j], 0, 0))
                  for j in range(n_pages)]
    return pl.pallas_call(
        functools.partial(_samp_fox_bias_kernel, n_pages=n_pages, n_new=n_new),
        grid_spec=pltpu.PrefetchScalarGridSpec(
            num_scalar_prefetch=1, grid=(bs,),
            in_specs=page_specs + [pl.BlockSpec((1, FORGET_GROUPS, PAGE), lambda b, pt: (b, 0, 0)),
                                   pl.BlockSpec((PAGE, PAGE), lambda b, pt: (0, 0))],
            out_specs=pl.BlockSpec((1, n_new * FORGET_GROUPS, lp), lambda b, pt: (b, 0, 0))),
        out_shape=jax.ShapeDtypeStruct((bs, n_new * FORGET_GROUPS, lp), F32),
        compiler_params=_cparams(("arbitrary",)),
        name="samp_fox_bias",
    )(page_table, *([pool_t] * n_pages), logf_new_t, tri)


def _samp_attn_kernel(*refs, kind, paged, n_kv, dk, dv, n_groups, lam_init):
    if paged:
        refs = refs[1:]
    q_ref = refs[0]
    kv_refs = refs[1:1 + n_kv]
    rest = refs[1 + n_kv:]
    if kind == "diff":
        new_ref, bias_ref, lam_ref, subln_ref, out_ref = rest
    elif kind == "fox":
        new_ref, bias_ref, out_ref = rest
    else:
        new_ref, bias_ref, gate_ref, out_ref = rest
    q = q_ref[0]
    tiles = []
    for r in kv_refs:
        blk = r[0, 0] if paged else r[0]
        for c in range(blk.shape[0] // PAGE):
            tiles.append(blk[c * PAGE:(c + 1) * PAGE])
    new = new_ref[0]
    tiles.append(jnp.concatenate([new, jnp.zeros((PAGE - new.shape[0], new.shape[1]), F32)], axis=0))
    scores = [_dot_nt(q, x[:, :dk].astype(BF16)) + bias_ref[0, :, j * PAGE:(j + 1) * PAGE]
              for j, x in enumerate(tiles)]
    m = functools.reduce(jnp.maximum, [jnp.max(s, axis=-1, keepdims=True) for s in scores])
    rows = q.shape[0]
    den = jnp.zeros((rows, 1), F32)
    acc = jnp.zeros((rows, n_groups * dv), F32)
    for s, x in zip(scores, tiles):
        p = jnp.exp(s - m) * (s > MASKED_BELOW).astype(F32)
        den = den + jnp.sum(p, axis=-1, keepdims=True)
        acc = acc + _dot(p.astype(BF16), x[:, dk:dk + n_groups * dv].astype(BF16))
    o = acc / jnp.maximum(den, 1e-30)
    if kind == "diff":
        lp = lam_ref[...]
        lam = (jnp.exp(jnp.sum(lp[0:1] * lp[1:2], axis=-1, keepdims=True))
               - jnp.exp(jnp.sum(lp[2:3] * lp[3:4], axis=-1, keepdims=True)) + lam_init)
        half = rows // 2
        a = o[:half] - lam * o[half:]
        rep = half // (SUBLANES * n_groups)
        a = _fold_groups(a, _row_group(half, n_groups * rep, rep), n_groups, dv)
        y = a * lax.rsqrt(jnp.mean(a * a, axis=-1, keepdims=True) + RMS_EPS) * subln_ref[...]
        out_ref[0] = y * (1.0 - lam_init)
    else:
        rep = rows // (SUBLANES * n_groups)
        folded = _fold_groups(o, _row_group(rows, n_groups * rep, rep), n_groups, dv)
        if kind == "fox":
            out_ref[0] = folded
        else:
            col = 1 if kind == "sel" else 2
            out_ref[0] = gate_ref[0][:, col:col + 1] * folded


def _samp_attn(kind, qbd, kv, layer, page_table, new_rows, bias, extras, *, dk, dv, n_groups, lam_init=0.0):
    bs, rows, _ = qbd.shape
    paged = page_table is not None
    w = kv.shape[-1]
    out_rows = rows // 2 if kind == "diff" else rows
    if paged:
        n_kv = page_table.shape[1]
        im = lambda f: (lambda b, pt: f(b))
        kv_specs = [pl.BlockSpec((1, 1, PAGE, w), lambda b, pt, j=j: (layer, pt[b, j], 0, 0)) for j in range(n_kv)]
        kv_args = [kv] * n_kv
    else:
        n_kv = 1
        im = lambda f: (lambda b: f(b))
        kv_specs = [pl.BlockSpec((1,) + kv.shape[1:], im(lambda b: (b, 0, 0)))]
        kv_args = [kv]
    per_b = lambda shape: pl.BlockSpec((1,) + shape[1:], im(lambda b: (b,) + (0,) * (len(shape) - 1)))
    const = lambda shape: pl.BlockSpec(shape, im(lambda b: (0,) * len(shape)))
    bias_spec = per_b(bias.shape) if bias.shape[0] == bs and bias.ndim == 3 and bias.shape[0] > 1 else const(bias.shape)
    in_specs = [per_b(qbd.shape)] + kv_specs + [per_b(new_rows.shape), bias_spec]
    if kind == "diff":
        in_specs += [const(extras[0].shape), const(extras[1].shape)]
    elif kind != "fox":
        in_specs.append(per_b(extras[0].shape))
    out_spec = pl.BlockSpec((1, out_rows, dv), im(lambda b: (b, 0, 0)))
    body = functools.partial(_samp_attn_kernel, kind=kind, paged=paged, n_kv=n_kv, dk=dk, dv=dv, n_groups=n_groups,
                             lam_init=lam_init)
    out_shape = jax.ShapeDtypeStruct((bs, out_rows, dv), F32)
    args = [qbd] + kv_args + [new_rows, bias] + list(extras)
    if paged:
        return pl.pallas_call(
            body,
            grid_spec=pltpu.PrefetchScalarGridSpec(num_scalar_prefetch=1, grid=(bs,), in_specs=in_specs,
                                                   out_specs=out_spec),
            out_shape=out_shape, compiler_params=_cparams(("arbitrary",)), name="samp_" + kind,
        )(page_table, *args)
    return pl.pallas_call(body, grid=(bs,), in_specs=in_specs, out_specs=out_spec, out_shape=out_shape,
                          compiler_params=_cparams(("arbitrary",)), name="samp_" + kind)(*args)


def _toeplitz_near(t, n_near):
    a = np.arange(t)
    return np.stack([(j * t + a[:, None] - a[None, :]) for j in range(n_near)])


def _prompt_tables(s, nsa_h, diff_h):
    tq = NSA_Q_TILE
    n_chunk = s // D_CMP
    front = n_chunk - SUBLANES
    tl = np.arange(tq)[:, None]
    w = np.arange(n_chunk)[None, :]
    d_cmp = tl - (D_CMP * (w - front) + L_CMP - 1)
    lw = np.arange(tq + WINDOW)[None, :]
    d_win = tl + WINDOW - lw
    d_win = np.where(d_win < WINDOW, d_win, -1)
    near = _toeplitz_near(FLASH_TILE, 2)
    return d_cmp, d_win, near


def kernel(x_prompt, x_sample, cache_nsa_cmp_kv, cache_nsa_sel_kv, state_nsa_win_kv, cache_fox_kv, cache_fox_logf,
           cache_diff_kv, page_table, rel_bias_table, norm_mix, norm_ffn, w_in, b_forget, qk_gain, nsa_cmp_w,
           nsa_cmp_pe, diff_lambda, diff_subln, w_branch, w_out, w_gate_up, w_down):
    bp, s, d_model = x_prompt.shape
    bs, t_new, _ = x_sample.shape
    depth = w_in.shape[0]
    bw = d_model // 2
    nsa_h = bw // HEAD_DIM
    fox_h = nsa_h
    diff_h = nsa_h // 2
    nsa_r = nsa_h // NSA_KVH
    fox_r = fox_h // FOX_KVH
    diff_r = diff_h // DIFF_KVH
    n_pages = page_table.shape[1]
    past = n_pages * PAGE
    win_buf = state_nsa_win_kv.shape[2]
    n_pool = cache_fox_kv.shape[1]
    assert t_new == SUBLANES and fox_h == FORGET_GROUPS and past % L_SEL == 0 and win_buf % PAGE == 0
    assert s % FLASH_TILE == 0 and s // L_SEL <= LANES
    kvw = NSA_KVH * HEAD_DIM
    tab_nsa = rel_bias_table[:, :nsa_h]
    tab_diff = rel_bias_table[:, nsa_h:]

    cols, small_pad, d_proj = _proj_layout(d_model)
    w_perm = jnp.pad(jnp.take(w_in, jnp.asarray(cols), axis=2), ((0, 0), (0, 0), (0, small_pad))).astype(BF16)
    ones = lambda n: jnp.ones((depth, n), F32)
    tile_gain = lambda gi, n_heads: jnp.tile(qk_gain[:, gi], (1, n_heads))
    gain_rows = jnp.concatenate(
        [tile_gain(0, nsa_h)]
        + [x for i in range(3) for x in (tile_gain(1 + i, NSA_KVH), ones(kvw))]
        + [tile_gain(4, fox_h), tile_gain(5, FOX_KVH), ones(FOX_KVH * HEAD_DIM),
           tile_gain(6, 2 * diff_h), tile_gain(7, 2 * DIFF_KVH), ones(DIFF_KVH * 2 * HEAD_DIM),
           ones(N_BRANCH * d_model + LANES)], axis=1)
    bf_rows = jnp.pad(b_forget, ((0, 0), (3 * nsa_h, LANES - 3 * nsa_h - fox_h)))
    m_blk = jnp.asarray(np.kron(np.eye(MXU_DIM // HEAD_DIM), np.ones((HEAD_DIM, HEAD_DIM))), BF16)
    w_br = w_branch.astype(BF16)
    w_o = w_out.astype(BF16)
    w_gu = w_gate_up.astype(BF16)
    w_dn = w_down.astype(BF16)
    slot_kv = np.repeat(np.arange(2), NSA_KVH)
    eye_slot = jnp.asarray(np.eye(2 * NSA_KVH), F32)
    w_slot = nsa_cmp_w[:, slot_kv]
    w_big = jnp.einsum('zsldq,st->zlsdtq', w_slot, eye_slot).reshape(depth, L_CMP, 2 * kvw, 2 * kvw)
    w_c1 = w_big[:, :D_CMP].reshape(depth, D_CMP * 2 * kvw, 2 * kvw).astype(BF16)
    w_c2 = w_big[:, D_CMP:].reshape(depth, D_CMP * 2 * kvw, 2 * kvw).astype(BF16)
    pe_slot = jnp.transpose(nsa_cmp_pe[:, slot_kv], (0, 2, 1, 3))
    pe_c1 = pe_slot[:, :D_CMP].reshape(depth, 1, D_CMP * 2 * kvw)
    pe_c2 = pe_slot[:, D_CMP:].reshape(depth, 1, D_CMP * 2 * kvw)

    t = FLASH_TILE
    d_cmp, d_win, d_near = _prompt_tables(s, nsa_h, diff_h)
    n_chunk = s // D_CMP
    rep_rows = lambda d, n: np.tile(d, (n, 1))
    cb_p = _t5_bias(rep_rows(d_cmp, nsa_h), np.repeat(np.arange(nsa_h), NSA_Q_TILE), tab_nsa
                    ).reshape(nsa_h, NSA_Q_TILE, n_chunk)
    wb_p = _t5_bias(rep_rows(d_win, nsa_h), np.repeat(np.arange(nsa_h), NSA_Q_TILE), tab_nsa
                    ).reshape(nsa_h, NSA_Q_TILE, NSA_Q_TILE + WINDOW)
    near2d = d_near.reshape(2 * t, t)
    near_sel = _t5_bias(rep_rows(near2d, nsa_h), np.repeat(np.arange(nsa_h), 2 * t), tab_nsa
                        ).reshape(NSA_KVH, nsa_r, 2, t, t)
    near_diff = _t5_bias(rep_rows(near2d, diff_h), np.repeat(np.arange(diff_h), 2 * t), tab_diff
                         ).reshape(DIFF_KVH, diff_r, 2, t, t)
    near_diff = jnp.repeat(near_diff, 2, axis=1)
    near_fox = jnp.asarray(np.where(d_near[:1] >= 0, 0.0, NEG_BIG), F32).reshape(1, 1, 1, t, t)
    onehot_blk = jnp.asarray(np.arange(s)[:, None] // L_SEL == np.arange(LANES)[None, :], BF16)

    lp_s = past + PAGE
    pos_q = past + np.arange(t_new)
    key_pos = np.arange(lp_s)
    key_ok = key_pos < past + t_new
    rows_n = t_new * nsa_h
    tok_of_row = np.repeat(np.arange(t_new), nsa_h)
    head_of_row = np.tile(np.arange(nsa_h), t_new)
    d_full = pos_q[tok_of_row][:, None] - key_pos[None, :]
    d_full = np.where(key_ok[None, :], d_full, -1)
    t5_sel_s = _t5_bias(d_full, head_of_row, tab_nsa)
    n_tok_s = past // D_CMP
    d_cmp_s = pos_q[tok_of_row][:, None] - (np.arange(n_tok_s) * D_CMP + L_CMP - 1)[None, :]
    cb_s = _t5_bias(d_cmp_s, head_of_row, tab_nsa)
    lw_s = win_buf + PAGE
    wpos = past - win_buf + np.arange(lw_s)
    d_win_s = pos_q[tok_of_row][:, None] - wpos[None, :]
    d_win_s = np.where((d_win_s < WINDOW) & (wpos < past + t_new)[None, :] & (wpos >= 0)[None, :], d_win_s, -1)
    wb_s = _t5_bias(d_win_s, head_of_row, tab_nsa)[None]
    dtok = np.tile(np.repeat(np.arange(t_new), diff_h), 2)
    dhead = np.tile(np.arange(diff_h), 2 * t_new)
    d_diff_s = np.where(key_ok[None, :], pos_q[dtok][:, None] - key_pos[None, :], -1)
    t5_diff_s = _t5_bias(d_diff_s, dhead, tab_diff)[None]
    ns_s = -(-(past + t_new) // L_SEL)
    cidx = np.arange(n_tok_s)[:, None]
    sidx = np.arange(LANES)[None, :]
    ov_s = jnp.asarray((cidx * D_CMP < (sidx + 1) * L_SEL) & (cidx * D_CMP + L_CMP > sidx * L_SEL), BF16)
    rsum_s = jnp.asarray(np.kron(np.eye(t_new * NSA_KVH), np.ones((nsa_r, nsa_r))), BF16)
    cur_s = jnp.asarray(np.broadcast_to((pos_q[tok_of_row] // L_SEL)[:, None], (rows_n, LANES)), jnp.int32)
    expand_s = jnp.asarray((key_pos[None, :] // L_SEL == np.arange(LANES)[:, None]) & key_ok[None, :], BF16)
    n_top_s = min(N_SEL, ns_s)

    eye_n = jnp.asarray(np.eye(NSA_KVH), BF16)
    eye_f = jnp.asarray(np.eye(FOX_KVH), BF16)
    eye_d = jnp.asarray(np.eye(DIFF_KVH * 2), BF16)

    pool_cmp_chunks = cache_nsa_cmp_kv.reshape(depth, n_pool, PAGE // D_CMP, D_CMP * 2 * kvw)
    pool_sel = cache_nsa_sel_kv.reshape(depth, n_pool, PAGE, 2 * kvw)
    pool_fox = cache_fox_kv.reshape(depth, n_pool, PAGE, 2 * FOX_KVH * HEAD_DIM)
    pool_diff = cache_diff_kv.reshape(depth, n_pool, PAGE, 2 * DIFF_KVH * 2 * HEAD_DIM)
    pool_logf_t = jnp.transpose(cache_fox_logf, (0, 1, 3, 2))
    win_state = state_nsa_win_kv.reshape(depth, bs, win_buf, 2 * kvw)

    xp = x_prompt.reshape(bp * s, d_model)
    xs = x_sample.reshape(bs * t_new, d_model)
    st = {n: [] for n in ('cmp_p', 'cmp_s', 'sel_p', 'sel_s', 'win_p', 'win_s', 'fkv_p', 'fkv_s',
                          'flf_p', 'flf_s', 'dkv_p', 'dkv_s')}
    n_gate = 3 * nsa_h

    for l in range(depth):
        lam_init = 0.8 - 0.6 * math.exp(-0.3 * l)
        lam_par = diff_lambda[l].astype(F32)
        subln = diff_subln[l].reshape(1, 2 * HEAD_DIM)
        proj_args = (norm_mix[l].reshape(1, d_model), w_perm[l], gain_rows[l].reshape(1, -1),
                     bf_rows[l].reshape(1, LANES), m_blk)

        nq, cmp_r, sel_r, win_r, fq, fox_rw, dq, diff_rw, mg, small = _project(xp, *proj_args)
        b3 = lambda a: a.reshape(bp, s, a.shape[-1])
        small3 = b3(small)
        tok = _compress_prompt(cmp_r.reshape(bp, n_chunk, D_CMP * 2 * kvw), pe_c1[l], pe_c2[l], w_c1[l], w_c2[l])
        win_pad = jnp.pad(b3(win_r).astype(BF16), ((0, 0), (WINDOW, 0), (0, 0)))
        o_cw, sel_mask = _nsa_local(b3(nq), tok, cb_p, win_pad, wb_p, small3)
        q_heads = jnp.transpose(b3(nq).reshape(bp, s, NSA_KVH, nsa_r, HEAD_DIM), (0, 2, 3, 1, 4))
        pad_q = jnp.zeros((bp, NSA_KVH, nsa_r, s, MXU_DIM - HEAD_DIM - LANES), BF16)
        q_sel = jnp.concatenate([q_heads, jnp.broadcast_to(sel_mask[:, :, None], (bp, NSA_KVH, nsa_r, s, LANES)),
                                 pad_q], axis=-1)
        sel_b = b3(sel_r).astype(BF16)
        k_sel = jnp.transpose(sel_b[..., :kvw].reshape(bp, s, NSA_KVH, HEAD_DIM), (0, 2, 1, 3))
        v_sel = jnp.transpose(sel_b[..., kvw:].reshape(bp, s, NSA_KVH, HEAD_DIM), (0, 2, 1, 3))
        k_sel = jnp.concatenate([k_sel, jnp.broadcast_to(onehot_blk, (bp, NSA_KVH, s, LANES)),
                                 jnp.zeros((bp, NSA_KVH, s, MXU_DIM - HEAD_DIM - LANES), BF16)], axis=-1)
        v_ones = lambda v, width: jnp.concatenate(
            [v, jnp.ones(v.shape[:-1] + (1,), BF16), jnp.zeros(v.shape[:-1] + (width - v.shape[-1] - 1,), BF16)], axis=-1)
        o_sel = _flash("sel", q_sel, k_sel, v_ones(v_sel, LANES), near_sel, (small3,), dv=HEAD_DIM, out_width=bw)
        c_all = _cumsum_rows(small3)[..., n_gate:n_gate + fox_h]
        c_g = jnp.transpose(c_all.reshape(bp, s, FOX_KVH, fox_r), (0, 2, 3, 1))
        c_parts = jnp.stack(_split3_trunc(c_g), axis=-1)
        fq_heads = jnp.transpose(b3(fq).reshape(bp, s, FOX_KVH, fox_r, HEAD_DIM), (0, 2, 3, 1, 4))
        eye_r = jnp.asarray(np.eye(fox_r), BF16)
        cq_cols = jnp.einsum('bgrsc,rq->bgrsqc', c_parts, eye_r).reshape(bp, FOX_KVH, fox_r, s, 3 * fox_r)
        one_cols = jnp.broadcast_to(jnp.repeat(eye_r, 3, axis=1)[None, None, :, None, :],
                                    (bp, FOX_KVH, fox_r, s, 3 * fox_r))
        n_aug = 6 * fox_r
        q_fox = jnp.concatenate([fq_heads, cq_cols, one_cols,
                                 jnp.zeros((bp, FOX_KVH, fox_r, s, LANES - HEAD_DIM - n_aug), BF16)], axis=-1)
        fox_b = b3(fox_rw)
        fkw = FOX_KVH * HEAD_DIM
        k_fox = jnp.transpose(fox_b[..., :fkw].astype(BF16).reshape(bp, s, FOX_KVH, HEAD_DIM), (0, 2, 1, 3))
        v_fox = jnp.transpose(fox_b[..., fkw:].astype(BF16).reshape(bp, s, FOX_KVH, HEAD_DIM), (0, 2, 1, 3))
        ck_cols = -jnp.transpose(c_parts, (0, 1, 3, 2, 4)).reshape(bp, FOX_KVH, s, 3 * fox_r)
        k_fox = jnp.concatenate([k_fox, jnp.ones((bp, FOX_KVH, s, 3 * fox_r), BF16), ck_cols,
                                 jnp.zeros((bp, FOX_KVH, s, LANES - HEAD_DIM - n_aug), BF16)], axis=-1)
        o_fox = _flash("fox", q_fox, k_fox, v_ones(v_fox, LANES), near_fox, (), dv=HEAD_DIM, out_width=bw)
        dq_h = jnp.transpose(b3(dq).reshape(bp, s, DIFF_KVH, diff_r, 2, HEAD_DIM), (0, 2, 3, 4, 1, 5))
        q_diff = jnp.einsum('bgrcsd,ce->bgrcsed', dq_h, jnp.asarray(np.eye(2), BF16)
                            ).reshape(bp, DIFF_KVH, diff_r * 2, s, 2 * HEAD_DIM)
        diff_b = b3(diff_rw).astype(BF16)
        dkw = DIFF_KVH * 2 * HEAD_DIM
        k_diff = jnp.transpose(diff_b[..., :dkw].reshape(bp, s, DIFF_KVH, 2 * HEAD_DIM), (0, 2, 1, 3))
        v_diff = jnp.transpose(diff_b[..., dkw:].reshape(bp, s, DIFF_KVH, 2 * HEAD_DIM), (0, 2, 1, 3))
        o_diff = _flash("diff", q_diff, k_diff, v_ones(v_diff, 2 * LANES), near_diff, (lam_par, subln),
                        dv=2 * HEAD_DIM, out_width=bw, lam_init=lam_init)
        flat = lambda a: a.reshape(bp * s, a.shape[-1])
        xp = _merge([flat(o_cw), flat(o_sel)], flat(o_fox), flat(o_diff), mg, xp, w_br[l], w_o[l])
        xp = _ffn(xp, norm_ffn[l].reshape(1, d_model), w_gu[l], w_dn[l])
        st['cmp_p'].append(cmp_r.reshape(bp, s, 2, NSA_KVH, HEAD_DIM))
        st['sel_p'].append(sel_r.reshape(bp, s, 2, NSA_KVH, HEAD_DIM))
        st['win_p'].append(b3(win_r)[:, s - min(WINDOW, s):].reshape(bp, min(WINDOW, s), 2, NSA_KVH, HEAD_DIM))
        st['fkv_p'].append(fox_rw.reshape(bp, s, 2, FOX_KVH, HEAD_DIM))
        st['flf_p'].append(small3[..., n_gate:n_gate + fox_h])
        st['dkv_p'].append(diff_rw.reshape(bp, s, 2, DIFF_KVH, 2 * HEAD_DIM))

        nq, cmp_r, sel_r, win_r, fq, fox_rw, dq, diff_rw, mg, small = _project(xs, *proj_args)
        s3 = lambda a: a.reshape(bs, t_new, a.shape[-1])
        small3 = s3(small)
        gates_s = jnp.pad(small3[..., :n_gate].reshape(bs, rows_n, 3), ((0, 0), (0, 0), (0, LANES - 3)))
        new_chunk = jnp.pad(cmp_r.reshape(bs, 1, t_new * 2 * kvw),
                            ((0, 0), (0, SUBLANES - 1), (0, (D_CMP - t_new) * 2 * kvw)))
        tok_s = _compress_sample(pool_cmp_chunks, l, page_table, new_chunk, pe_c1[l], pe_c2[l], w_c1[l], w_c2[l])
        qn_bd = jnp.einsum('btgrd,gh->btgrhd', s3(nq).reshape(bs, t_new, NSA_KVH, nsa_r, HEAD_DIM), eye_n
                           ).reshape(bs, rows_n, kvw)
        o_c_s, bias_sel = _samp_cmp(qn_bd, tok_s, cb_s, ov_s, rsum_s, cur_s, expand_s, t5_sel_s, gates_s, n_top_s)
        o_s_s = _samp_attn("sel", qn_bd, pool_sel, l, page_table, s3(sel_r), bias_sel, (gates_s,),
                           dk=kvw, dv=HEAD_DIM, n_groups=NSA_KVH)
        o_w_s = _samp_attn("win", qn_bd, win_state[l], l, None, s3(win_r), wb_s, (gates_s,),
                           dk=kvw, dv=HEAD_DIM, n_groups=NSA_KVH)
        logf_new_t = jnp.pad(jnp.transpose(small3[..., n_gate:n_gate + fox_h], (0, 2, 1)),
                             ((0, 0), (0, 0), (0, PAGE - t_new)))
        bias_fox = _samp_fox_bias(pool_logf_t, l, page_table, logf_new_t)
        fq_bd = jnp.einsum('btgrd,gh->btgrhd', s3(fq).reshape(bs, t_new, FOX_KVH, fox_r, HEAD_DIM), eye_f
                           ).reshape(bs, t_new * fox_h, FOX_KVH * HEAD_DIM)
        o_f_s = _samp_attn("fox", fq_bd, pool_fox, l, page_table, s3(fox_rw), bias_fox, (),
                           dk=FOX_KVH * HEAD_DIM, dv=HEAD_DIM, n_groups=FOX_KVH)
        dq_s = jnp.transpose(s3(dq).reshape(bs, t_new, DIFF_KVH, diff_r, 2, HEAD_DIM), (0, 4, 1, 2, 3, 5))
        dq_bd = jnp.einsum('bctgrd,gch->bctgrhd', dq_s, eye_d.reshape(DIFF_KVH, 2, DIFF_KVH * 2)
                           ).reshape(bs, 2 * t_new * diff_h, DIFF_KVH * 2 * HEAD_DIM)
        o_d_s = _samp_attn("diff", dq_bd, pool_diff, l, page_table, s3(diff_rw), t5_diff_s, (lam_par, subln),
                           dk=DIFF_KVH * 2 * HEAD_DIM, dv=2 * HEAD_DIM, n_groups=DIFF_KVH, lam_init=lam_init)
        flat_s = lambda a: a.reshape(bs * t_new, bw)
        xs = _merge([flat_s(o_c_s), flat_s(o_s_s), flat_s(o_w_s)], flat_s(o_f_s), flat_s(o_d_s), mg, xs,
                    w_br[l], w_o[l])
        xs = _ffn(xs, norm_ffn[l].reshape(1, d_model), w_gu[l], w_dn[l])
        win_full = jnp.concatenate([win_state[l], s3(win_r)], axis=1)
        st['cmp_s'].append(cmp_r.reshape(bs, t_new, 2, NSA_KVH, HEAD_DIM))
        st['sel_s'].append(sel_r.reshape(bs, t_new, 2, NSA_KVH, HEAD_DIM))
        st['win_s'].append(win_full[:, t_new:].reshape(bs, win_buf, 2, NSA_KVH, HEAD_DIM))
        st['fkv_s'].append(fox_rw.reshape(bs, t_new, 2, FOX_KVH, HEAD_DIM))
        st['flf_s'].append(small3[..., n_gate:n_gate + fox_h])
        st['dkv_s'].append(diff_rw.reshape(bs, t_new, 2, DIFF_KVH, 2 * HEAD_DIM))

    ns = {n: jnp.stack(v, axis=0) for n, v in st.items()}
    return (xp.reshape(bp, s, d_model), xs.reshape(bs, t_new, d_model),
            ns['cmp_p'], ns['cmp_s'], ns['sel_p'], ns['sel_s'], ns['win_p'], ns['win_s'],
            ns['fkv_p'], ns['fkv_s'], ns['flf_p'], ns['flf_s'], ns['dkv_p'], ns['dkv_s'])
```

```python
import functools
import math

import numpy as np
import jax
import jax.numpy as jnp
from jax import lax
from jax.experimental import pallas as pl
from jax.experimental.pallas import tpu as pltpu

F32 = jnp.float32
BF16 = jnp.bfloat16

HEAD_DIM = 64
NSA_KVH = 2
FOX_KVH = 4
DIFF_KVH = 2
N_BRANCH = 3
L_CMP = 32
D_CMP = 16
L_SEL = 64
N_SEL = 16
WINDOW = 512
N_BUCKETS = 32
MAX_DISTANCE = 128
FORGET_GROUPS = 8
RMS_EPS = 1e-6
NEG_BIG = -1e30
MASKED_BELOW = -1e29

LANES = 128
SUBLANES = 8
MXU_DIM = 256
VMEM_LIMIT = 56 * 1024 * 1024

ROW_TILE = 256
NSA_Q_TILE = 128
FLASH_TILE = 512
CUMSUM_TILE = 512
PAGE = 128


def _cparams(sem):
    return pltpu.CompilerParams(dimension_semantics=sem, vmem_limit_bytes=VMEM_LIMIT)


def _resident(shape):
    zeros = (0,) * len(shape)
    return pl.BlockSpec(shape, lambda *_: zeros)


def _dot(a, b):
    return jnp.dot(a, b, preferred_element_type=F32)


def _dot_nt(a, b):
    return lax.dot_general(a, b, (((1,), (1,)), ((), ())), preferred_element_type=F32)


def _split3(x):
    hi = x.astype(BF16)
    r = x - hi.astype(F32)
    mid = r.astype(BF16)
    lo = (r - mid.astype(F32)).astype(BF16)
    return hi, mid, lo


def _split2(x):
    hi = x.astype(BF16)
    return hi, (x - hi.astype(F32)).astype(BF16)


def _split3_trunc(x):
    def top(v):
        bits = lax.bitcast_convert_type(v, jnp.uint32) & jnp.uint32(0xFFFF0000)
        return lax.bitcast_convert_type(bits, F32)
    hi = top(x)
    mid = top(x - hi)
    lo = top(x - hi - mid)
    return hi.astype(BF16), mid.astype(BF16), lo.astype(BF16)


def _masked_softmax(s, valid):
    s = jnp.where(valid, s, NEG_BIG)
    m = jnp.max(s, axis=-1, keepdims=True)
    p = jnp.exp(s - m) * valid.astype(F32)
    return p / jnp.maximum(jnp.sum(p, axis=-1, keepdims=True), 1e-30)


def _t5_thresholds():
    n = np.arange(0, 4 * MAX_DISTANCE)
    exact = N_BUCKETS // 2
    nf = np.maximum(n, 1).astype(np.float32)
    large = exact + (np.log(nf / np.float32(exact)) / np.float32(math.log(MAX_DISTANCE / exact))
                     * np.float32(N_BUCKETS - exact)).astype(np.int32)
    b = np.where(n < exact, n, np.minimum(large, N_BUCKETS - 1))
    assert np.all(np.diff(b) >= 0) and b[-1] == N_BUCKETS - 1
    return tuple(int(np.argmax(b >= k)) for k in range(1, N_BUCKETS))


_T5_THR = _t5_thresholds()


def _t5_kernel(dist_ref, tab_ref, out_ref):
    d = dist_ref[...]
    n = jnp.maximum(d, 0)
    tab = tab_ref[...]
    val = jnp.broadcast_to(tab[:, 0:1], d.shape)
    for k in range(1, N_BUCKETS):
        val = jnp.where(n >= _T5_THR[k - 1], tab[:, k:k + 1], val)
    out_ref[...] = jnp.where(d < 0, NEG_BIG, val - tab[:, N_BUCKETS - 1:N_BUCKETS])


def _t5_bias(dist, head_of_row, table):
    rows, cols = dist.shape
    blk = math.gcd(rows, 256)
    tab_rows = jnp.transpose(table)[jnp.asarray(head_of_row, jnp.int32)]
    return pl.pallas_call(
        _t5_kernel,
        grid=(rows // blk,),
        in_specs=[pl.BlockSpec((blk, cols), lambda i: (i, 0)),
                  pl.BlockSpec((blk, N_BUCKETS), lambda i: (i, 0))],
        out_specs=pl.BlockSpec((blk, cols), lambda i: (i, 0)),
        out_shape=jax.ShapeDtypeStruct((rows, cols), F32),
        compiler_params=_cparams(("arbitrary",)),
        name="t5_bias",
    )(jnp.asarray(dist, jnp.int32), tab_rows.astype(F32))


def _proj_layout(d_model):
    nsa_h = (d_model // 2) // HEAD_DIM
    splits = (nsa_h * HEAD_DIM, 6 * NSA_KVH * HEAD_DIM, 3 * nsa_h,
              nsa_h * HEAD_DIM, 2 * FOX_KVH * HEAD_DIM, nsa_h,
              (nsa_h // 2) * 2 * HEAD_DIM, 2 * DIFF_KVH * 2 * HEAD_DIM, N_BRANCH * d_model)
    off = np.concatenate([[0], np.cumsum(splits)])
    nq, nkv, ng, fq, fkv, ff, dq, dkv, mg = [np.arange(off[i], off[i + 1]) for i in range(9)]
    small_pad = LANES - len(ng) - len(ff)
    cols = np.concatenate([nq, nkv, fq, fkv, dq, dkv, mg, ng, ff])
    return cols, small_pad, int(off[-1])


def _proj_kernel(x_ref, g_ref, w_ref, gain_ref, bf_ref, m_ref,
                 nq_ref, cmp_ref, sel_ref, win_ref, fq_ref, fox_ref, dq_ref, diff_ref, mg_ref, small_ref,
                 *, d_model):
    bw = d_model // 2
    x = x_ref[...]
    xn = x * lax.rsqrt(jnp.mean(x * x, axis=-1, keepdims=True) + RMS_EPS) * g_ref[...]
    xb = xn.astype(BF16)

    def seg(a, b):
        return _dot(xb, w_ref[:, a:b])

    def head_norm(p, a):
        outs = []
        wdt = p.shape[1]
        step = min(wdt, MXU_DIM)
        for c in range(0, wdt, step):
            pc = p[:, c:c + step]
            ms = _dot((pc * pc).astype(BF16), m_ref[:step, :step]) * (1.0 / HEAD_DIM)
            outs.append(pc * lax.rsqrt(ms + RMS_EPS) * gain_ref[:, a + c:a + c + step])
        return outs[0] if len(outs) == 1 else jnp.concatenate(outs, axis=1)

    scale = HEAD_DIM ** -0.5
    o = 0
    nq_ref[...] = (head_norm(seg(o, o + bw), o) * scale).astype(BF16)
    o += bw
    kvw = NSA_KVH * HEAD_DIM
    for ref in (cmp_ref, sel_ref, win_ref):
        p = seg(o, o + 2 * kvw)
        ref[...] = jnp.concatenate([head_norm(p[:, :kvw], o), p[:, kvw:]], axis=1)
        o += 2 * kvw
    fq_ref[...] = (head_norm(seg(o, o + bw), o) * scale).astype(BF16)
    o += bw
    fkw = FOX_KVH * HEAD_DIM
    p = seg(o, o + 2 * fkw)
    fox_ref[...] = jnp.concatenate([head_norm(p[:, :fkw], o), p[:, fkw:]], axis=1)
    o += 2 * fkw
    dq_ref[...] = (head_norm(seg(o, o + bw), o) * scale).astype(BF16)
    o += bw
    dkw = DIFF_KVH * 2 * HEAD_DIM
    p = seg(o, o + 2 * dkw)
    diff_ref[...] = jnp.concatenate([head_norm(p[:, :dkw], o), p[:, dkw:]], axis=1)
    o += 2 * dkw
    chunk = 2 * MXU_DIM
    for c in range(0, N_BRANCH * d_model, chunk):
        mg_ref[:, c:c + chunk] = jax.nn.sigmoid(seg(o + c, o + c + chunk))
    o += N_BRANCH * d_model
    p = seg(o, o + LANES)
    n_gate = 3 * (bw // HEAD_DIM)
    lane = lax.broadcasted_iota(jnp.int32, p.shape, 1)
    z = p + bf_ref[...]
    logf = jnp.minimum(z, 0.0) - jnp.log1p(jnp.exp(-jnp.abs(z)))
    small_ref[...] = jnp.where(lane < n_gate, jax.nn.sigmoid(p),
                               jnp.where(lane < n_gate + FORGET_GROUPS, logf, 0.0))


def _project(x2d, g_norm, w_perm, gain_row, bf_row, m_blk):
    n, d_model = x2d.shape
    bw = d_model // 2
    wtot = w_perm.shape[1]
    tm = min(ROW_TILE, n)
    kvw = 2 * NSA_KVH * HEAD_DIM
    widths = [(bw, BF16), (kvw, F32), (kvw, F32), (kvw, F32), (bw, BF16), (2 * FOX_KVH * HEAD_DIM, F32),
              (bw, BF16), (2 * DIFF_KVH * 2 * HEAD_DIM, F32), (N_BRANCH * d_model, F32), (LANES, F32)]
    row = lambda w: pl.BlockSpec((tm, w), lambda i: (i, 0))
    return pl.pallas_call(
        functools.partial(_proj_kernel, d_model=d_model),
        grid=(n // tm,),
        in_specs=[row(d_model), _resident((1, d_model)), _resident((d_model, wtot)), _resident((1, wtot)),
                  _resident((1, LANES)), _resident((MXU_DIM, MXU_DIM))],
        out_specs=[row(w) for w, _ in widths],
        out_shape=[jax.ShapeDtypeStruct((n, w), dt) for w, dt in widths],
        compiler_params=_cparams(("arbitrary",)),
        name="proj",
    )(x2d, g_norm, w_perm, gain_row, bf_row, m_blk)


def _cumsum_kernel(x_ref, tri_ref, out_ref, carry_ref):
    @pl.when(pl.program_id(1) == 0)
    def _():
        carry_ref[...] = jnp.zeros_like(carry_ref)

    tri = tri_ref[...]
    hi, mid, lo = _split3(x_ref[0])
    cs = _dot(tri, hi) + _dot(tri, mid) + _dot(tri, lo) + carry_ref[...]
    out_ref[0] = cs
    carry_ref[...] = cs[cs.shape[0] - 1:, :]


def _cumsum_rows(x):
    b, s, w = x.shape
    t = min(CUMSUM_TILE, s)
    tri = jnp.asarray(np.tril(np.ones((t, t), np.float32)), BF16)
    return pl.pallas_call(
        _cumsum_kernel,
        grid=(b, s // t),
        in_specs=[pl.BlockSpec((1, t, w), lambda i, j: (i, j, 0)), _resident((t, t))],
        out_specs=pl.BlockSpec((1, t, w), lambda i, j: (i, j, 0)),
        out_shape=jax.ShapeDtypeStruct((b, s, w), F32),
        scratch_shapes=[pltpu.VMEM((1, w), F32)],
        compiler_params=_cparams(("arbitrary", "arbitrary")),
        name="cumsum",
    )(x, tri)


def _compress_tokens(x, pe1, pe2, w1, w2):
    first = _dot((x + pe1).astype(BF16), w1)
    second = _dot((x + pe2).astype(BF16), w2)
    n = x.shape[0]
    return first + pltpu.roll(second, shift=n - 1, axis=0)


def _compress_prompt_kernel(x_ref, pe1_ref, pe2_ref, w1_ref, w2_ref, out_ref, *, front):
    tok = _compress_tokens(x_ref[0], pe1_ref[...], pe2_ref[...], w1_ref[...], w2_ref[...])
    n, w = tok.shape
    out_ref[0, 0:front, :] = jnp.zeros((front, w), F32)
    out_ref[0, front:front + n, :] = tok
    out_ref[0, front + n:, :] = jnp.zeros((SUBLANES, w), F32)


def _compress_prompt(x, pe1, pe2, w1, w2):
    b, n, cw = x.shape
    w = w1.shape[1]
    front = n - SUBLANES
    return pl.pallas_call(
        functools.partial(_compress_prompt_kernel, front=front),
        grid=(b,),
        in_specs=[pl.BlockSpec((1, n, cw), lambda i: (i, 0, 0)), _resident((1, cw)), _resident((1, cw)),
                  _resident((cw, w)), _resident((cw, w))],
        out_specs=pl.BlockSpec((1, 2 * n, w), lambda i: (i, 0, 0)),
        out_shape=jax.ShapeDtypeStruct((b, 2 * n, w), F32),
        compiler_params=_cparams(("arbitrary",)),
        name="compress_prompt",
    )(x, pe1, pe2, w1, w2)


def _compress_sample_kernel(pt_ref, *refs, n_pages):
    del pt_ref
    page_refs = refs[:n_pages]
    new_ref, pe1_ref, pe2_ref, w1_ref, w2_ref, out_ref, rows_ref = refs[n_pages:]
    chunks = []
    for r in page_refs:
        n_half = rows_ref.shape[0]
        for h in range(n_half):
            rows_ref[h] = jnp.transpose(r[0, 0, h * LANES:(h + 1) * LANES, :])
        chunks.append(jnp.concatenate([rows_ref[h, pl.ds(l, PAGE // D_CMP, stride=D_CMP), :]
                                       for l in range(D_CMP) for h in range(n_half)], axis=1))
    x = jnp.concatenate(chunks + [new_ref[0]], axis=0)
    tok = _compress_tokens(x, pe1_ref[...], pe2_ref[...], w1_ref[...], w2_ref[...])
    out_ref[0] = tok[:out_ref.shape[1], :]


def _compress_sample(pool_t, layer, page_table, new_chunk, pe1, pe2, w1, w2):
    bs, n_pages = page_table.shape
    feat = pool_t.shape[2]
    cw = new_chunk.shape[2]
    w = w1.shape[1]
    n_tok = n_pages * (PAGE // D_CMP)
    page_specs = [pl.BlockSpec((1, 1, feat, PAGE), lambda b, pt, j=j: (layer, pt[b, j], 0, 0))
                  for j in range(n_pages)]
    const = lambda shape: pl.BlockSpec(shape, lambda b, pt: (0,) * len(shape))
    return pl.pallas_call(
        functools.partial(_compress_sample_kernel, n_pages=n_pages),
        grid_spec=pltpu.PrefetchScalarGridSpec(
            num_scalar_prefetch=1, grid=(bs,),
            in_specs=page_specs + [pl.BlockSpec((1, SUBLANES, cw), lambda b, pt: (b, 0, 0)),
                                   const((1, cw)), const((1, cw)), const((cw, w)), const((cw, w))],
            out_specs=pl.BlockSpec((1, n_tok, w), lambda b, pt: (b, 0, 0)),
            scratch_shapes=[pltpu.VMEM((feat // LANES, PAGE, LANES), F32)]),
        out_shape=jax.ShapeDtypeStruct((bs, n_tok, w), F32),
        compiler_params=_cparams(("arbitrary",)),
        name="compress_sample",
    )(page_table, *([pool_t] * n_pages), new_chunk, pe1, pe2, w1, w2)


def _select_top_blocks(score_t, n_top):
    nb = score_t.shape[0]
    n_f = lax.broadcasted_iota(jnp.int32, score_t.shape, 0).astype(F32)
    taken = jnp.zeros(score_t.shape, F32)
    for _ in range(n_top):
        free = taken < 0.5
        cand = jnp.where(free, score_t, -jnp.inf)
        m = jnp.max(cand, axis=0, keepdims=True)
        idx = jnp.min(jnp.where(free & (cand == m), n_f, float(nb)), axis=0, keepdims=True)
        taken = jnp.where(n_f == idx, 1.0, taken)
    return taken


def _nsa_local_kernel(q_ref, tok_ref, cb_ref, win_ref, wb_ref, small_ref, out_ref, sel_ref, *, n_chunk, n_top):
    tq = NSA_Q_TILE
    i = pl.program_id(1)
    t0 = i * tq
    n_heads = cb_ref.shape[0]
    rep = n_heads // NSA_KVH
    q = q_ref[0]
    gates = small_ref[0]
    tok = tok_ref[0, pl.ds(pl.multiple_of(i * (tq // D_CMP), SUBLANES), n_chunk), :]
    wrows = win_ref[0, pl.ds(pl.multiple_of(t0, tq), tq + WINDOW), :]
    kvw = NSA_KVH * HEAD_DIM
    front = n_chunk - SUBLANES
    w_io = lax.broadcasted_iota(jnp.int32, (tq, n_chunk), 1)
    c_exists = w_io >= front - i * (tq // D_CMP)
    lw_io = lax.broadcasted_iota(jnp.int32, (tq, tq + WINDOW), 1)
    w_exists = lw_io >= WINDOW - t0
    n_io = lax.broadcasted_iota(jnp.int32, (LANES, n_chunk), 0)
    c_io = lax.broadcasted_iota(jnp.int32, (LANES, n_chunk), 1) + (i * (tq // D_CMP) - front)
    ov_t = jnp.where((c_io * D_CMP < (n_io + 1) * L_SEL) & (c_io * D_CMP + L_CMP > n_io * L_SEL), 1.0, 0.0).astype(BF16)
    blk = lax.broadcasted_iota(jnp.int32, (LANES, tq), 0)
    cur = 2 * i + (lax.broadcasted_iota(jnp.int32, (LANES, tq), 1) >= L_SEL).astype(jnp.int32)
    for g in range(NSA_KVH):
        kc = tok[:, g * HEAD_DIM:(g + 1) * HEAD_DIM].astype(BF16)
        vc = tok[:, kvw + g * HEAD_DIM:kvw + (g + 1) * HEAD_DIM].astype(BF16)
        kw = wrows[:, g * HEAD_DIM:(g + 1) * HEAD_DIM]
        vw = wrows[:, kvw + g * HEAD_DIM:kvw + (g + 1) * HEAD_DIM]
        psum = jnp.zeros((tq, n_chunk), F32)
        for r in range(rep):
            h = g * rep + r
            qh = q[:, h * HEAD_DIM:(h + 1) * HEAD_DIM]
            cb = cb_ref[h]
            p_c = _masked_softmax(_dot_nt(qh, kc) + cb, (cb > MASKED_BELOW) & c_exists)
            o_c = _dot(p_c.astype(BF16), vc)
            psum = psum + p_c
            wb = wb_ref[h]
            p_w = _masked_softmax(_dot_nt(qh, kw) + wb, (wb > MASKED_BELOW) & w_exists)
            o_w = _dot(p_w.astype(BF16), vw)
            out_ref[0, :, h * HEAD_DIM:(h + 1) * HEAD_DIM] = (gates[:, 3 * h:3 * h + 1] * o_c
                                                             + gates[:, 3 * h + 2:3 * h + 3] * o_w)
        p_hi, p_lo = _split2(psum)
        imp_t = _dot_nt(ov_t, p_hi) + _dot_nt(ov_t, p_lo)
        forced = (blk == 0) | (blk == cur) | (blk == cur - 1)
        score = jnp.where(blk > cur, -jnp.inf, jnp.where(forced, jnp.inf, imp_t))
        taken = _select_top_blocks(score, n_top)
        sel_ref[0, g] = jnp.where(jnp.transpose(taken) > 0.5, 0.0, NEG_BIG).astype(BF16)


def _nsa_local(q, tok, cb, win, wb, small):
    b, s, qw = q.shape
    n_chunk = tok.shape[1] // 2
    n_heads = cb.shape[0]
    tq = NSA_Q_TILE
    n_top = min(N_SEL, s // L_SEL)
    return pl.pallas_call(
        functools.partial(_nsa_local_kernel, n_chunk=n_chunk, n_top=n_top),
        grid=(b, s // tq),
        in_specs=[pl.BlockSpec((1, tq, qw), lambda bi, i: (bi, i, 0)),
                  pl.BlockSpec((1,) + tok.shape[1:], lambda bi, i: (bi, 0, 0)),
                  _resident(cb.shape),
                  pl.BlockSpec((1,) + win.shape[1:], lambda bi, i: (bi, 0, 0)),
                  _resident(wb.shape),
                  pl.BlockSpec((1, tq, LANES), lambda bi, i: (bi, i, 0))],
        out_specs=[pl.BlockSpec((1, tq, qw), lambda bi, i: (bi, i, 0)),
                   pl.BlockSpec((1, NSA_KVH, tq, LANES), lambda bi, i: (bi, 0, i, 0))],
        out_shape=[jax.ShapeDtypeStruct((b, s, qw), F32),
                   jax.ShapeDtypeStruct((b, NSA_KVH, s, LANES), BF16)],
        compiler_params=_cparams(("arbitrary", "arbitrary")),
        name="nsa_local",
    )(q, tok, cb, win, wb, small)


def _flash_kernel(*refs, kind, n_units, dv, n_near, lam_init, gate_base):
    if kind == "diff":
        q_ref, k_ref, v_ref, near_ref, lam_ref, subln_ref, out_ref, m_ref, acc_ref = refs
    elif kind == "sel":
        q_ref, k_ref, v_ref, near_ref, small_ref, out_ref, m_ref, acc_ref = refs
    else:
        q_ref, k_ref, v_ref, near_ref, out_ref, m_ref, acc_ref = refs
    t = FLASH_TILE
    g = pl.program_id(1)
    i = pl.program_id(2)
    m_ref[...] = jnp.full(m_ref.shape, NEG_BIG, F32)
    acc_ref[...] = jnp.zeros(acc_ref.shape, F32)

    def step(j, near_idx):
        start = pl.multiple_of(j * t, t)
        kt = k_ref[0, 0, pl.ds(start, t), :]
        vt = v_ref[0, 0, pl.ds(start, t), :]
        for u in range(n_units):
            s = _dot_nt(q_ref[0, 0, u], kt)
            if near_idx is not None:
                s = s + near_ref[0, u if near_ref.shape[1] > 1 else 0, near_idx]
            m_old = m_ref[u]
            m_new = jnp.maximum(m_old, jnp.max(s, axis=-1, keepdims=True))
            p = jnp.exp(s - m_new)
            acc_ref[u] = acc_ref[u] * jnp.exp(m_old - m_new) + _dot(p.astype(BF16), vt)
            m_ref[u] = m_new

    def far_body(j, carry):
        step(j, None)
        return carry

    lax.fori_loop(0, jnp.maximum(i - (n_near - 1), 0), far_body, 0)
    if n_near == 2:
        @pl.when(i >= 1)
        def _():
            step(i - 1, 1)
    step(i, 0)

    def unit_out(u):
        acc = acc_ref[u]
        return acc[:, :dv] / acc[:, dv:dv + 1]

    if kind == "diff":
        lp = lam_ref[...]
        lam = (jnp.exp(jnp.sum(lp[0:1] * lp[1:2], axis=-1, keepdims=True))
               - jnp.exp(jnp.sum(lp[2:3] * lp[3:4], axis=-1, keepdims=True)) + lam_init)
        for r in range(n_units // 2):
            o = unit_out(2 * r) - lam * unit_out(2 * r + 1)
            y = o * lax.rsqrt(jnp.mean(o * o, axis=-1, keepdims=True) + RMS_EPS) * subln_ref[...]
            out_ref[0, :, r * dv:(r + 1) * dv] = y * (1.0 - lam_init)
    elif kind == "sel":
        gates = small_ref[0]
        for u in range(n_units):
            col = 3 * (g * n_units + u) + gate_base
            lane = lax.broadcasted_iota(jnp.int32, gates.shape, 1)
            gate = jnp.sum(jnp.where(lane == col, gates, 0.0), axis=-1, keepdims=True)
            out_ref[0, :, u * dv:(u + 1) * dv] = gate * unit_out(u)
    else:
        for u in range(n_units):
            out_ref[0, :, u * dv:(u + 1) * dv] = unit_out(u)


def _flash(kind, q, k, v, near, extras, *, dv, out_width, lam_init=0.0):
    b, n_g, n_units, s, dk = q.shape
    dva = v.shape[-1]
    t = FLASH_TILE
    n_near = near.shape[2]
    wg = out_width // n_g
    in_specs = [pl.BlockSpec((1, 1, n_units, t, dk), lambda bi, g, i: (bi, g, 0, i, 0)),
                pl.BlockSpec((1, 1, s, dk), lambda bi, g, i: (bi, g, 0, 0)),
                pl.BlockSpec((1, 1, s, dva), lambda bi, g, i: (bi, g, 0, 0))]
    if near.shape[0] > 1:
        in_specs.append(pl.BlockSpec((1,) + near.shape[1:], lambda bi, g, i: (g, 0, 0, 0, 0)))
    else:
        in_specs.append(_resident(near.shape))
    if kind == "diff":
        in_specs += [_resident(extras[0].shape), _resident(extras[1].shape)]
    elif kind == "sel":
        in_specs.append(pl.BlockSpec((1, t, LANES), lambda bi, g, i: (bi, i, 0)))
    return pl.pallas_call(
        functools.partial(_flash_kernel, kind=kind, n_units=n_units, dv=dv, n_near=n_near, lam_init=lam_init,
                          gate_base=1),
        grid=(b, n_g, s // t),
        in_specs=in_specs,
        out_specs=pl.BlockSpec((1, t, wg), lambda bi, g, i: (bi, i, g)),
        out_shape=jax.ShapeDtypeStruct((b, s, out_width), F32),
        scratch_shapes=[pltpu.VMEM((n_units, t, 1), F32), pltpu.VMEM((n_units, t, dva), F32)],
        compiler_params=_cparams(("arbitrary", "arbitrary", "arbitrary")),
        name="flash_" + kind,
    )(q, k, v, near, *extras)


def _merge_kernel(*refs, n_nsa):
    nsa_refs = refs[:n_nsa]
    fox_ref, diff_ref, mg_ref, x_ref, wbr_ref, wo_ref, out_ref = refs[n_nsa:]
    d = x_ref.shape[1]
    o_a = nsa_refs[0][...]
    for r in nsa_refs[1:]:
        o_a = o_a + r[...]
    mix = jnp.zeros(x_ref.shape, F32)
    for n, o in enumerate((o_a, fox_ref[...], diff_ref[...])):
        mix = mix + mg_ref[:, n * d:(n + 1) * d] * _dot(o.astype(BF16), wbr_ref[n])
    out_ref[...] = x_ref[...] + _dot(mix.astype(BF16), wo_ref[...])


def _merge(nsa_parts, o_fox, o_diff, mg, x2d, w_br, w_o):
    n, d = x2d.shape
    bw = o_fox.shape[1]
    tm = min(ROW_TILE, n)
    row = lambda w: pl.BlockSpec((tm, w), lambda i: (i, 0))
    return pl.pallas_call(
        functools.partial(_merge_kernel, n_nsa=len(nsa_parts)),
        grid=(n // tm,),
        in_specs=[row(bw)] * (len(nsa_parts) + 2) + [row(N_BRANCH * d), row(d), _resident(w_br.shape),
                                                     _resident(w_o.shape)],
        out_specs=row(d),
        out_shape=jax.ShapeDtypeStruct((n, d), F32),
        compiler_params=_cparams(("arbitrary",)),
        name="merge",
    )(*nsa_parts, o_fox, o_diff, mg, x2d, w_br, w_o)


def _ffn_kernel(x_ref, g_ref, wgu_ref, wd_ref, out_ref, *, d_ff):
    x = x_ref[...]
    h = (x * lax.rsqrt(jnp.mean(x * x, axis=-1, keepdims=True) + RMS_EPS) * g_ref[...]).astype(BF16)
    acc = x
    for c in range(0, d_ff, MXU_DIM):
        gate = _dot(h, wgu_ref[:, c:c + MXU_DIM])
        up = _dot(h, wgu_ref[:, d_ff + c:d_ff + c + MXU_DIM])
        acc = acc + _dot((jax.nn.silu(gate) * up).astype(BF16), wd_ref[c:c + MXU_DIM, :])
    out_ref[...] = acc


def _ffn(x2d, g_norm, w_gu, w_d):
    n, d = x2d.shape
    d_ff = w_d.shape[0]
    tm = min(ROW_TILE, n)
    row = pl.BlockSpec((tm, d), lambda i: (i, 0))
    return pl.pallas_call(
        functools.partial(_ffn_kernel, d_ff=d_ff),
        grid=(n // tm,),
        in_specs=[row, _resident((1, d)), _resident(w_gu.shape), _resident(w_d.shape)],
        out_specs=row,
        out_shape=jax.ShapeDtypeStruct((n, d), F32),
        compiler_params=_cparams(("arbitrary",)),
        name="ffn",
    )(x2d, g_norm, w_gu, w_d)


def _fold_groups(o, group_of_row, n_groups, width):
    out = jnp.zeros((o.shape[0], width), F32)
    for gi in range(n_groups):
        out = out + jnp.where(group_of_row == gi, o[:, gi * width:(gi + 1) * width], 0.0)
    return out


def _row_group(rows, period, per_group):
    r = lax.broadcasted_iota(jnp.int32, (rows, 1), 0)
    return (r % period) // per_group


def _samp_cmp_kernel(q_ref, tok_ref, cb_ref, ov_ref, rsum_ref, cur_ref, expand_ref, t5_ref, gate_ref,
                     out_ref, bias_ref, *, n_top):
    q = q_ref[0]
    tok = tok_ref[0]
    kvw = NSA_KVH * HEAD_DIM
    cb = cb_ref[...]
    p = _masked_softmax(_dot_nt(q, tok[:, :kvw].astype(BF16)) + cb, cb > MASKED_BELOW)
    o = _dot(p.astype(BF16), tok[:, kvw:].astype(BF16))
    rows = q.shape[0]
    rep = rows // (SUBLANES * NSA_KVH)
    grp = _row_group(rows, NSA_KVH * rep, rep)
    out_ref[0] = gate_ref[0][:, 0:1] * _fold_groups(o, grp, NSA_KVH, HEAD_DIM)
    p_hi, p_lo = _split2(p)
    psum = _dot(rsum_ref[...], p_hi) + _dot(rsum_ref[...], p_lo)
    s_hi, s_lo = _split2(psum)
    imp = _dot(s_hi, ov_ref[...]) + _dot(s_lo, ov_ref[...])
    blk = lax.broadcasted_iota(jnp.int32, imp.shape, 1)
    cur = cur_ref[...]
    forced = (blk == 0) | (blk == cur) | (blk == cur - 1)
    score = jnp.where(blk > cur, -jnp.inf, jnp.where(forced, jnp.inf, imp))
    n_f = blk.astype(F32)
    taken = jnp.zeros(imp.shape, F32)
    for _ in range(n_top):
        free = taken < 0.5
        cand = jnp.where(free, score, -jnp.inf)
        m = jnp.max(cand, axis=-1, keepdims=True)
        idx = jnp.min(jnp.where(free & (cand == m), n_f, float(imp.shape[1])), axis=-1, keepdims=True)
        taken = jnp.where(n_f == idx, 1.0, taken)
    key_sel = _dot(taken.astype(BF16), expand_ref[...])
    bias_ref[0] = jnp.where(key_sel > 0.5, t5_ref[...], NEG_BIG)


def _samp_cmp(qbd, tok, cb, ov, rsum, cur, expand, t5, gate, n_top):
    bs, rows, dk = qbd.shape
    lp = expand.shape[1]
    per_b = lambda shape: pl.BlockSpec((1,) + shape[1:], lambda b: (b,) + (0,) * (len(shape) - 1))
    return pl.pallas_call(
        functools.partial(_samp_cmp_kernel, n_top=n_top),
        grid=(bs,),
        in_specs=[per_b(qbd.shape), per_b(tok.shape), _resident(cb.shape), _resident(ov.shape), _resident(rsum.shape),
                  _resident(cur.shape), _resident(expand.shape), _resident(t5.shape), per_b(gate.shape)],
        out_specs=[pl.BlockSpec((1, rows, HEAD_DIM), lambda b: (b, 0, 0)),
                   pl.BlockSpec((1, rows, lp), lambda b: (b, 0, 0))],
        out_shape=[jax.ShapeDtypeStruct((bs, rows, HEAD_DIM), F32), jax.ShapeDtypeStruct((bs, rows, lp), F32)],
        compiler_params=_cparams(("arbitrary",)),
        name="samp_cmp",
    )(qbd, tok, cb, ov, rsum, cur, expand, t5, gate)


def _samp_fox_bias_kernel(pt_ref, *refs, n_pages, n_new):
    del pt_ref
    page_refs = refs[:n_pages]
    new_ref, tri_ref, out_ref = refs[n_pages:]
    tri = tri_ref[...]

    def lane_cumsum(x):
        hi, mid, lo = _split3(x)
        return _dot(hi, tri) + _dot(mid, tri) + _dot(lo, tri)

    carry = jnp.zeros((FORGET_GROUPS, 1), F32)
    c_tiles = []
    for r in page_refs:
        c = lane_cumsum(r[0, 0]) + carry
        carry = c[:, PAGE - 1:PAGE]
        c_tiles.append(c)
    c_new = lane_cumsum(new_ref[0]) + carry
    lane = lax.broadcasted_iota(jnp.int32, (FORGET_GROUPS, PAGE), 1)
    for t in range(n_new):
        c_q = c_new[:, t:t + 1]
        lo_r, hi_r = t * FORGET_GROUPS, (t + 1) * FORGET_GROUPS
        for j, c in enumerate(c_tiles):
            out_ref[0, lo_r:hi_r, j * PAGE:(j + 1) * PAGE] = c_q - c
        out_ref[0, lo_r:hi_r, n_pages * PAGE:] = jnp.where(lane <= t, c_q - c_new, NEG_BIG)


def _samp_fox_bias(pool_t, layer, page_table, logf_new_t):
    bs, n_pages = page_table.shape
    n_new = SUBLANES
    lp = (n_pages + 1) * PAGE
    tri = jnp.asarray(np.triu(np.ones((PAGE, PAGE), np.float32)), BF16)
    page_specs = [pl.BlockSpec((1, 1, FORGET_GROUPS, PAGE), lambda b, pt, j=j: (layer, pt[b, j], 0, 0))
                  for j in range(n_pages)]
    return pl.pallas_call(
        functools.partial(_samp_fox_bias_kernel, n_pages=n_pages, n_new=n_new),
        grid_spec=pltpu.PrefetchScalarGridSpec(
            num_scalar_prefetch=1, grid=(bs,),
            in_specs=page_specs + [pl.BlockSpec((1, FORGET_GROUPS, PAGE), lambda b, pt: (b, 0, 0)),
                                   pl.BlockSpec((PAGE, PAGE), lambda b, pt: (0, 0))],
            out_specs=pl.BlockSpec((1, n_new * FORGET_GROUPS, lp), lambda b, pt: (b, 0, 0))),
        out_shape=jax.ShapeDtypeStruct((bs, n_new * FORGET_GROUPS, lp), F32),
        compiler_params=_cparams(("arbitrary",)),
        name="samp_fox_bias",
    )(page_table, *([pool_t] * n_pages), logf_new_t, tri)


def _samp_attn_kernel(*refs, kind, paged, layout, n_kv, dk, dv, n_groups, lam_init):
    if paged:
        refs = refs[1:]
    q_ref = refs[0]
    kv_refs = refs[1:1 + n_kv]
    rest = refs[1 + n_kv:]
    if kind == "diff":
        new_ref, bias_ref, lam_ref, subln_ref, out_ref = rest
    elif kind == "fox":
        new_ref, bias_ref, out_ref = rest
    else:
        new_ref, bias_ref, gate_ref, out_ref = rest
    q = q_ref[0]
    wv = n_groups * dv
    tiles = []
    for r in kv_refs:
        if layout == "transposed":
            tiles.append((r[0, 0, 0:dk, :].astype(BF16), r[0, 0, dk:dk + wv, :].astype(BF16), True))
        elif layout == "interleaved":
            sl = lambda o: [r[0, 0, pl.ds(o + gi, PAGE, stride=2 * n_groups), :] for gi in range(n_groups)]
            tiles.append((jnp.concatenate(sl(0), axis=1).astype(BF16),
                          jnp.concatenate(sl(n_groups), axis=1).astype(BF16), False))
        else:
            for c in range(r.shape[1] // PAGE):
                x = r[0, c * PAGE:(c + 1) * PAGE, :]
                tiles.append((x[:, :dk].astype(BF16), x[:, dk:dk + wv].astype(BF16), False))
    new = new_ref[0]
    new = jnp.concatenate([new, jnp.zeros((PAGE - new.shape[0], new.shape[1]), F32)], axis=0)
    tiles.append((new[:, :dk].astype(BF16), new[:, dk:dk + wv].astype(BF16), False))
    scores = [(_dot(q, k) if tr else _dot_nt(q, k)) + bias_ref[0, :, j * PAGE:(j + 1) * PAGE]
              for j, (k, _, tr) in enumerate(tiles)]
    m = functools.reduce(jnp.maximum, [jnp.max(s, axis=-1, keepdims=True) for s in scores])
    rows = q.shape[0]
    den = jnp.zeros((rows, 1), F32)
    acc = jnp.zeros((rows, wv), F32)
    for s, (_, v, tr) in zip(scores, tiles):
        p = jnp.exp(s - m) * (s > MASKED_BELOW).astype(F32)
        den = den + jnp.sum(p, axis=-1, keepdims=True)
        pb = p.astype(BF16)
        acc = acc + (_dot_nt(pb, v) if tr else _dot(pb, v))
    o = acc / jnp.maximum(den, 1e-30)
    if kind == "diff":
        lp = lam_ref[...]
        lam = (jnp.exp(jnp.sum(lp[0:1] * lp[1:2], axis=-1, keepdims=True))
               - jnp.exp(jnp.sum(lp[2:3] * lp[3:4], axis=-1, keepdims=True)) + lam_init)
        half = rows // 2
        a = o[:half] - lam * o[half:]
        rep = half // (SUBLANES * n_groups)
        a = _fold_groups(a, _row_group(half, n_groups * rep, rep), n_groups, dv)
        y = a * lax.rsqrt(jnp.mean(a * a, axis=-1, keepdims=True) + RMS_EPS) * subln_ref[...]
        out_ref[0] = y * (1.0 - lam_init)
    else:
        rep = rows // (SUBLANES * n_groups)
        folded = _fold_groups(o, _row_group(rows, n_groups * rep, rep), n_groups, dv)
        if kind == "fox":
            out_ref[0] = folded
        else:
            col = 1 if kind == "sel" else 2
            out_ref[0] = gate_ref[0][:, col:col + 1] * folded


def _samp_attn(kind, qbd, kv, layer, page_table, new_rows, bias, extras, *, dk, dv, n_groups, layout="rows",
               lam_init=0.0):
    bs, rows, _ = qbd.shape
    paged = page_table is not None
    out_rows = rows // 2 if kind == "diff" else rows
    if paged:
        n_kv = page_table.shape[1]
        im = lambda f: (lambda b, pt: f(b))
        kv_specs = [pl.BlockSpec((1, 1) + kv.shape[2:], lambda b, pt, j=j: (layer, pt[b, j], 0, 0))
                    for j in range(n_kv)]
        kv_args = [kv] * n_kv
    else:
        n_kv = 1
        im = lambda f: (lambda b: f(b))
        kv_specs = [pl.BlockSpec((1,) + kv.shape[1:], im(lambda b: (b, 0, 0)))]
        kv_args = [kv]
    per_b = lambda shape: pl.BlockSpec((1,) + shape[1:], im(lambda b: (b,) + (0,) * (len(shape) - 1)))
    const = lambda shape: pl.BlockSpec(shape, im(lambda b: (0,) * len(shape)))
    bias_spec = per_b(bias.shape) if bias.shape[0] == bs and bias.ndim == 3 and bias.shape[0] > 1 else const(bias.shape)
    in_specs = [per_b(qbd.shape)] + kv_specs + [per_b(new_rows.shape), bias_spec]
    if kind == "diff":
        in_specs += [const(extras[0].shape), const(extras[1].shape)]
    elif kind != "fox":
        in_specs.append(per_b(extras[0].shape))
    out_spec = pl.BlockSpec((1, out_rows, dv), im(lambda b: (b, 0, 0)))
    body = functools.partial(_samp_attn_kernel, kind=kind, paged=paged, layout=layout, n_kv=n_kv, dk=dk, dv=dv,
                             n_groups=n_groups, lam_init=lam_init)
    out_shape = jax.ShapeDtypeStruct((bs, out_rows, dv), F32)
    args = [qbd] + kv_args + [new_rows, bias] + list(extras)
    if paged:
        return pl.pallas_call(
            body,
            grid_spec=pltpu.PrefetchScalarGridSpec(num_scalar_prefetch=1, grid=(bs,), in_specs=in_specs,
                                                   out_specs=out_spec),
            out_shape=out_shape, compiler_params=_cparams(("arbitrary",)), name="samp_" + kind,
        )(page_table, *args)
    return pl.pallas_call(body, grid=(bs,), in_specs=in_specs, out_specs=out_spec, out_shape=out_shape,
                          compiler_params=_cparams(("arbitrary",)), name="samp_" + kind)(*args)


def _toeplitz_near(t, n_near):
    a = np.arange(t)
    return np.stack([(j * t + a[:, None] - a[None, :]) for j in range(n_near)])


def _prompt_tables(s, nsa_h, diff_h):
    tq = NSA_Q_TILE
    n_chunk = s // D_CMP
    front = n_chunk - SUBLANES
    tl = np.arange(tq)[:, None]
    w = np.arange(n_chunk)[None, :]
    d_cmp = tl - (D_CMP * (w - front) + L_CMP - 1)
    lw = np.arange(tq + WINDOW)[None, :]
    d_win = tl + WINDOW - lw
    d_win = np.where(d_win < WINDOW, d_win, -1)
    near = _toeplitz_near(FLASH_TILE, 2)
    return d_cmp, d_win, near


def kernel(x_prompt, x_sample, cache_nsa_cmp_kv, cache_nsa_sel_kv, state_nsa_win_kv, cache_fox_kv, cache_fox_logf,
           cache_diff_kv, page_table, rel_bias_table, norm_mix, norm_ffn, w_in, b_forget, qk_gain, nsa_cmp_w,
           nsa_cmp_pe, diff_lambda, diff_subln, w_branch, w_out, w_gate_up, w_down):
    bp, s, d_model = x_prompt.shape
    bs, t_new, _ = x_sample.shape
    depth = w_in.shape[0]
    bw = d_model // 2
    nsa_h = bw // HEAD_DIM
    fox_h = nsa_h
    diff_h = nsa_h // 2
    nsa_r = nsa_h // NSA_KVH
    fox_r = fox_h // FOX_KVH
    diff_r = diff_h // DIFF_KVH
    n_pages = page_table.shape[1]
    past = n_pages * PAGE
    win_buf = state_nsa_win_kv.shape[2]
    n_pool = cache_fox_kv.shape[1]
    assert t_new == SUBLANES and fox_h == FORGET_GROUPS and past % L_SEL == 0 and win_buf % PAGE == 0
    assert s % FLASH_TILE == 0 and s // L_SEL <= LANES
    kvw = NSA_KVH * HEAD_DIM
    tab_nsa = rel_bias_table[:, :nsa_h]
    tab_diff = rel_bias_table[:, nsa_h:]

    cols, small_pad, d_proj = _proj_layout(d_model)
    w_perm = jnp.pad(jnp.take(w_in, jnp.asarray(cols), axis=2), ((0, 0), (0, 0), (0, small_pad))).astype(BF16)
    ones = lambda n: jnp.ones((depth, n), F32)
    tile_gain = lambda gi, n_heads: jnp.tile(qk_gain[:, gi], (1, n_heads))
    gain_rows = jnp.concatenate(
        [tile_gain(0, nsa_h)]
        + [x for i in range(3) for x in (tile_gain(1 + i, NSA_KVH), ones(kvw))]
        + [tile_gain(4, fox_h), tile_gain(5, FOX_KVH), ones(FOX_KVH * HEAD_DIM),
           tile_gain(6, 2 * diff_h), tile_gain(7, 2 * DIFF_KVH), ones(DIFF_KVH * 2 * HEAD_DIM),
           ones(N_BRANCH * d_model + LANES)], axis=1)
    bf_rows = jnp.pad(b_forget, ((0, 0), (3 * nsa_h, LANES - 3 * nsa_h - fox_h)))
    m_blk = jnp.asarray(np.kron(np.eye(MXU_DIM // HEAD_DIM), np.ones((HEAD_DIM, HEAD_DIM))), BF16)
    w_br = w_branch.astype(BF16)
    w_o = w_out.astype(BF16)
    w_gu = w_gate_up.astype(BF16)
    w_dn = w_down.astype(BF16)
    slot_kv = np.repeat(np.arange(2), NSA_KVH)
    eye_slot = jnp.asarray(np.eye(2 * NSA_KVH), F32)
    w_slot = nsa_cmp_w[:, slot_kv]
    w_big = jnp.einsum('zsldq,st->zlsdtq', w_slot, eye_slot).reshape(depth, L_CMP, 2 * kvw, 2 * kvw)
    w_c1 = w_big[:, :D_CMP].reshape(depth, D_CMP * 2 * kvw, 2 * kvw).astype(BF16)
    w_c2 = w_big[:, D_CMP:].reshape(depth, D_CMP * 2 * kvw, 2 * kvw).astype(BF16)
    pe_slot = jnp.transpose(nsa_cmp_pe[:, slot_kv], (0, 2, 1, 3))
    pe_c1 = pe_slot[:, :D_CMP].reshape(depth, 1, D_CMP * 2 * kvw)
    pe_c2 = pe_slot[:, D_CMP:].reshape(depth, 1, D_CMP * 2 * kvw)

    t = FLASH_TILE
    d_cmp, d_win, d_near = _prompt_tables(s, nsa_h, diff_h)
    n_chunk = s // D_CMP
    rep_rows = lambda d, n: np.tile(d, (n, 1))
    cb_p = _t5_bias(rep_rows(d_cmp, nsa_h), np.repeat(np.arange(nsa_h), NSA_Q_TILE), tab_nsa
                    ).reshape(nsa_h, NSA_Q_TILE, n_chunk)
    wb_p = _t5_bias(rep_rows(d_win, nsa_h), np.repeat(np.arange(nsa_h), NSA_Q_TILE), tab_nsa
                    ).reshape(nsa_h, NSA_Q_TILE, NSA_Q_TILE + WINDOW)
    near2d = d_near.reshape(2 * t, t)
    near_sel = _t5_bias(rep_rows(near2d, nsa_h), np.repeat(np.arange(nsa_h), 2 * t), tab_nsa
                        ).reshape(NSA_KVH, nsa_r, 2, t, t)
    near_diff = _t5_bias(rep_rows(near2d, diff_h), np.repeat(np.arange(diff_h), 2 * t), tab_diff
                         ).reshape(DIFF_KVH, diff_r, 2, t, t)
    near_diff = jnp.repeat(near_diff, 2, axis=1)
    near_fox = jnp.asarray(np.where(d_near[:1] >= 0, 0.0, NEG_BIG), F32).reshape(1, 1, 1, t, t)
    onehot_blk = jnp.asarray(np.arange(s)[:, None] // L_SEL == np.arange(LANES)[None, :], BF16)

    lp_s = past + PAGE
    pos_q = past + np.arange(t_new)
    key_pos = np.arange(lp_s)
    key_ok = key_pos < past + t_new
    rows_n = t_new * nsa_h
    tok_of_row = np.repeat(np.arange(t_new), nsa_h)
    head_of_row = np.tile(np.arange(nsa_h), t_new)
    d_full = pos_q[tok_of_row][:, None] - key_pos[None, :]
    d_full = np.where(key_ok[None, :], d_full, -1)
    t5_sel_s = _t5_bias(d_full, head_of_row, tab_nsa)
    n_tok_s = past // D_CMP
    d_cmp_s = pos_q[tok_of_row][:, None] - (np.arange(n_tok_s) * D_CMP + L_CMP - 1)[None, :]
    cb_s = _t5_bias(d_cmp_s, head_of_row, tab_nsa)
    lw_s = win_buf + PAGE
    wpos = past - win_buf + np.arange(lw_s)
    d_win_s = pos_q[tok_of_row][:, None] - wpos[None, :]
    d_win_s = np.where((d_win_s < WINDOW) & (wpos < past + t_new)[None, :] & (wpos >= 0)[None, :], d_win_s, -1)
    wb_s = _t5_bias(d_win_s, head_of_row, tab_nsa)[None]
    dtok = np.tile(np.repeat(np.arange(t_new), diff_h), 2)
    dhead = np.tile(np.arange(diff_h), 2 * t_new)
    d_diff_s = np.where(key_ok[None, :], pos_q[dtok][:, None] - key_pos[None, :], -1)
    t5_diff_s = _t5_bias(d_diff_s, dhead, tab_diff)[None]
    ns_s = -(-(past + t_new) // L_SEL)
    cidx = np.arange(n_tok_s)[:, None]
    sidx = np.arange(LANES)[None, :]
    ov_s = jnp.asarray((cidx * D_CMP < (sidx + 1) * L_SEL) & (cidx * D_CMP + L_CMP > sidx * L_SEL), BF16)
    rsum_s = jnp.asarray(np.kron(np.eye(t_new * NSA_KVH), np.ones((nsa_r, nsa_r))), BF16)
    cur_s = jnp.asarray(np.broadcast_to((pos_q[tok_of_row] // L_SEL)[:, None], (rows_n, LANES)), jnp.int32)
    expand_s = jnp.asarray((key_pos[None, :] // L_SEL == np.arange(LANES)[:, None]) & key_ok[None, :], BF16)
    n_top_s = min(N_SEL, ns_s)

    eye_n = jnp.asarray(np.eye(NSA_KVH), BF16)
    eye_f = jnp.asarray(np.eye(FOX_KVH), BF16)
    eye_d = jnp.asarray(np.eye(DIFF_KVH * 2), BF16)

    pages_t = lambda c: jnp.transpose(c, (0, 1, 3, 4, 5, 2)).reshape(depth, n_pool, -1, PAGE)
    pool_cmp = pages_t(cache_nsa_cmp_kv)
    pool_sel = pages_t(cache_nsa_sel_kv)
    pool_fox = pages_t(cache_fox_kv)
    pool_diff = cache_diff_kv.reshape(depth, n_pool, PAGE * 2 * DIFF_KVH, 2 * HEAD_DIM)
    pool_logf_t = jnp.transpose(cache_fox_logf, (0, 1, 3, 2))
    win_state = state_nsa_win_kv.reshape(depth, bs, win_buf, 2 * kvw)

    xp = x_prompt.reshape(bp * s, d_model)
    xs = x_sample.reshape(bs * t_new, d_model)
    st = {n: [] for n in ('cmp_p', 'cmp_s', 'sel_p', 'sel_s', 'win_p', 'win_s', 'fkv_p', 'fkv_s',
                          'flf_p', 'flf_s', 'dkv_p', 'dkv_s')}
    n_gate = 3 * nsa_h

    for l in range(depth):
        lam_init = 0.8 - 0.6 * math.exp(-0.3 * l)
        lam_par = diff_lambda[l].astype(F32)
        subln = diff_subln[l].reshape(1, 2 * HEAD_DIM)
        proj_args = (norm_mix[l].reshape(1, d_model), w_perm[l], gain_rows[l].reshape(1, -1),
                     bf_rows[l].reshape(1, LANES), m_blk)

        nq, cmp_r, sel_r, win_r, fq, fox_rw, dq, diff_rw, mg, small = _project(xp, *proj_args)
        b3 = lambda a: a.reshape(bp, s, a.shape[-1])
        small3 = b3(small)
        tok = _compress_prompt(cmp_r.reshape(bp, n_chunk, D_CMP * 2 * kvw), pe_c1[l], pe_c2[l], w_c1[l], w_c2[l])
        win_pad = jnp.pad(b3(win_r).astype(BF16), ((0, 0), (WINDOW, 0), (0, 0)))
        o_cw, sel_mask = _nsa_local(b3(nq), tok, cb_p, win_pad, wb_p, small3)
        q_heads = jnp.transpose(b3(nq).reshape(bp, s, NSA_KVH, nsa_r, HEAD_DIM), (0, 2, 3, 1, 4))
        pad_q = jnp.zeros((bp, NSA_KVH, nsa_r, s, MXU_DIM - HEAD_DIM - LANES), BF16)
        q_sel = jnp.concatenate([q_heads, jnp.broadcast_to(sel_mask[:, :, None], (bp, NSA_KVH, nsa_r, s, LANES)),
                                 pad_q], axis=-1)
        sel_b = b3(sel_r).astype(BF16)
        k_sel = jnp.transpose(sel_b[..., :kvw].reshape(bp, s, NSA_KVH, HEAD_DIM), (0, 2, 1, 3))
        v_sel = jnp.transpose(sel_b[..., kvw:].reshape(bp, s, NSA_KVH, HEAD_DIM), (0, 2, 1, 3))
        k_sel = jnp.concatenate([k_sel, jnp.broadcast_to(onehot_blk, (bp, NSA_KVH, s, LANES)),
                                 jnp.zeros((bp, NSA_KVH, s, MXU_DIM - HEAD_DIM - LANES), BF16)], axis=-1)
        v_ones = lambda v, width: jnp.concatenate(
            [v, jnp.ones(v.shape[:-1] + (1,), BF16), jnp.zeros(v.shape[:-1] + (width - v.shape[-1] - 1,), BF16)], axis=-1)
        o_sel = _flash("sel", q_sel, k_sel, v_ones(v_sel, LANES), near_sel, (small3,), dv=HEAD_DIM, out_width=bw)
        c_all = _cumsum_rows(small3)[..., n_gate:n_gate + fox_h]
        c_g = jnp.transpose(c_all.reshape(bp, s, FOX_KVH, fox_r), (0, 2, 3, 1))
        c_parts = jnp.stack(_split3_trunc(c_g), axis=-1)
        fq_heads = jnp.transpose(b3(fq).reshape(bp, s, FOX_KVH, fox_r, HEAD_DIM), (0, 2, 3, 1, 4))
        eye_r = jnp.asarray(np.eye(fox_r), BF16)
        cq_cols = jnp.einsum('bgrsc,rq->bgrsqc', c_parts, eye_r).reshape(bp, FOX_KVH, fox_r, s, 3 * fox_r)
        one_cols = jnp.broadcast_to(jnp.repeat(eye_r, 3, axis=1)[None, None, :, None, :],
                                    (bp, FOX_KVH, fox_r, s, 3 * fox_r))
        n_aug = 6 * fox_r
        q_fox = jnp.concatenate([fq_heads, cq_cols, one_cols,
                                 jnp.zeros((bp, FOX_KVH, fox_r, s, LANES - HEAD_DIM - n_aug), BF16)], axis=-1)
        fox_b = b3(fox_rw)
        fkw = FOX_KVH * HEAD_DIM
        k_fox = jnp.transpose(fox_b[..., :fkw].astype(BF16).reshape(bp, s, FOX_KVH, HEAD_DIM), (0, 2, 1, 3))
        v_fox = jnp.transpose(fox_b[..., fkw:].astype(BF16).reshape(bp, s, FOX_KVH, HEAD_DIM), (0, 2, 1, 3))
        ck_cols = -jnp.transpose(c_parts, (0, 1, 3, 2, 4)).reshape(bp, FOX_KVH, s, 3 * fox_r)
        k_fox = jnp.concatenate([k_fox, jnp.ones((bp, FOX_KVH, s, 3 * fox_r), BF16), ck_cols,
                                 jnp.zeros((bp, FOX_KVH, s, LANES - HEAD_DIM - n_aug), BF16)], axis=-1)
        o_fox = _flash("fox", q_fox, k_fox, v_ones(v_fox, LANES), near_fox, (), dv=HEAD_DIM, out_width=bw)
        dq_h = jnp.transpose(b3(dq).reshape(bp, s, DIFF_KVH, diff_r, 2, HEAD_DIM), (0, 2, 3, 4, 1, 5))
        q_diff = jnp.einsum('bgrcsd,ce->bgrcsed', dq_h, jnp.asarray(np.eye(2), BF16)
                            ).reshape(bp, DIFF_KVH, diff_r * 2, s, 2 * HEAD_DIM)
        diff_b = b3(diff_rw).astype(BF16)
        dkw = DIFF_KVH * 2 * HEAD_DIM
        k_diff = jnp.transpose(diff_b[..., :dkw].reshape(bp, s, DIFF_KVH, 2 * HEAD_DIM), (0, 2, 1, 3))
        v_diff = jnp.transpose(diff_b[..., dkw:].reshape(bp, s, DIFF_KVH, 2 * HEAD_DIM), (0, 2, 1, 3))
        o_diff = _flash("diff", q_diff, k_diff, v_ones(v_diff, 2 * LANES), near_diff, (lam_par, subln),
                        dv=2 * HEAD_DIM, out_width=bw, lam_init=lam_init)
        flat = lambda a: a.reshape(bp * s, a.shape[-1])
        xp = _merge([flat(o_cw), flat(o_sel)], flat(o_fox), flat(o_diff), mg, xp, w_br[l], w_o[l])
        xp = _ffn(xp, norm_ffn[l].reshape(1, d_model), w_gu[l], w_dn[l])
        st['cmp_p'].append(cmp_r.reshape(bp, s, 2, NSA_KVH, HEAD_DIM))
        st['sel_p'].append(sel_r.reshape(bp, s, 2, NSA_KVH, HEAD_DIM))
        st['win_p'].append(b3(win_r)[:, s - min(WINDOW, s):].reshape(bp, min(WINDOW, s), 2, NSA_KVH, HEAD_DIM))
        st['fkv_p'].append(fox_rw.reshape(bp, s, 2, FOX_KVH, HEAD_DIM))
        st['flf_p'].append(small3[..., n_gate:n_gate + fox_h])
        st['dkv_p'].append(diff_rw.reshape(bp, s, 2, DIFF_KVH, 2 * HEAD_DIM))

        nq, cmp_r, sel_r, win_r, fq, fox_rw, dq, diff_rw, mg, small = _project(xs, *proj_args)
        s3 = lambda a: a.reshape(bs, t_new, a.shape[-1])
        small3 = s3(small)
        gates_s = jnp.pad(small3[..., :n_gate].reshape(bs, rows_n, 3), ((0, 0), (0, 0), (0, LANES - 3)))
        new_chunk = jnp.pad(cmp_r.reshape(bs, 1, t_new * 2 * kvw),
                            ((0, 0), (0, SUBLANES - 1), (0, (D_CMP - t_new) * 2 * kvw)))
        tok_s = _compress_sample(pool_cmp, l, page_table, new_chunk, pe_c1[l], pe_c2[l], w_c1[l], w_c2[l])
        qn_bd = jnp.einsum('btgrd,gh->btgrhd', s3(nq).reshape(bs, t_new, NSA_KVH, nsa_r, HEAD_DIM), eye_n
                           ).reshape(bs, rows_n, kvw)
        o_c_s, bias_sel = _samp_cmp(qn_bd, tok_s, cb_s, ov_s, rsum_s, cur_s, expand_s, t5_sel_s, gates_s, n_top_s)
        o_s_s = _samp_attn("sel", qn_bd, pool_sel, l, page_table, s3(sel_r), bias_sel, (gates_s,),
                           dk=kvw, dv=HEAD_DIM, n_groups=NSA_KVH, layout="transposed")
        o_w_s = _samp_attn("win", qn_bd, win_state[l], l, None, s3(win_r), wb_s, (gates_s,),
                           dk=kvw, dv=HEAD_DIM, n_groups=NSA_KVH)
        logf_new_t = jnp.pad(jnp.transpose(small3[..., n_gate:n_gate + fox_h], (0, 2, 1)),
                             ((0, 0), (0, 0), (0, PAGE - t_new)))
        bias_fox = _samp_fox_bias(pool_logf_t, l, page_table, logf_new_t)
        fq_bd = jnp.einsum('btgrd,gh->btgrhd', s3(fq).reshape(bs, t_new, FOX_KVH, fox_r, HEAD_DIM), eye_f
                           ).reshape(bs, t_new * fox_h, FOX_KVH * HEAD_DIM)
        o_f_s = _samp_attn("fox", fq_bd, pool_fox, l, page_table, s3(fox_rw), bias_fox, (),
                           dk=FOX_KVH * HEAD_DIM, dv=HEAD_DIM, n_groups=FOX_KVH, layout="transposed")
        dq_s = jnp.transpose(s3(dq).reshape(bs, t_new, DIFF_KVH, diff_r, 2, HEAD_DIM), (0, 4, 1, 2, 3, 5))
        dq_bd = jnp.einsum('bctgrd,gch->bctgrhd', dq_s, eye_d.reshape(DIFF_KVH, 2, DIFF_KVH * 2)
                           ).reshape(bs, 2 * t_new * diff_h, DIFF_KVH * 2 * HEAD_DIM)
        o_d_s = _samp_attn("diff", dq_bd, pool_diff, l, page_table, s3(diff_rw), t5_diff_s, (lam_par, subln),
                           dk=DIFF_KVH * 2 * HEAD_DIM, dv=2 * HEAD_DIM, n_groups=DIFF_KVH, layout="interleaved",
                           lam_init=lam_init)
        flat_s = lambda a: a.reshape(bs * t_new, bw)
        xs = _merge([flat_s(o_c_s), flat_s(o_s_s), flat_s(o_w_s)], flat_s(o_f_s), flat_s(o_d_s), mg, xs,
                    w_br[l], w_o[l])
        xs = _ffn(xs, norm_ffn[l].reshape(1, d_model), w_gu[l], w_dn[l])
        win_full = jnp.concatenate([win_state[l], s3(win_r)], axis=1)
        st['cmp_s'].append(cmp_r.reshape(bs, t_new, 2, NSA_KVH, HEAD_DIM))
        st['sel_s'].append(sel_r.reshape(bs, t_new, 2, NSA_KVH, HEAD_DIM))
        st['win_s'].append(win_full[:, t_new:].reshape(bs, win_buf, 2, NSA_KVH, HEAD_DIM))
        st['fkv_s'].append(fox_rw.reshape(bs, t_new, 2, FOX_KVH, HEAD_DIM))
        st['flf_s'].append(small3[..., n_gate:n_gate + fox_h])
        st['dkv_s'].append(diff_rw.reshape(bs, t_new, 2, DIFF_KVH, 2 * HEAD_DIM))

    ns = {n: jnp.stack(v, axis=0) for n, v in st.items()}
    return (xp.reshape(bp, s, d_model), xs.reshape(bs, t_new, d_model),
            ns['cmp_p'], ns['cmp_s'], ns['sel_p'], ns['sel_s'], ns['win_p'], ns['win_s'],
            ns['fkv_p'], ns['fkv_s'], ns['flf_p'], ns['flf_s'], ns['dkv_p'], ns['dkv_s'])
```

```python
import functools
import math

import numpy as np
import jax
import jax.numpy as jnp
from jax import lax
from jax.experimental import pallas as pl
from jax.experimental.pallas import tpu as pltpu

F32 = jnp.float32
BF16 = jnp.bfloat16

HEAD_DIM = 64
NSA_KVH = 2
FOX_KVH = 4
DIFF_KVH = 2
N_BRANCH = 3
L_CMP = 32
D_CMP = 16
L_SEL = 64
N_SEL = 16
WINDOW = 512
N_BUCKETS = 32
MAX_DISTANCE = 128
FORGET_GROUPS = 8
RMS_EPS = 1e-6
NEG_BIG = -1e30
MASKED_BELOW = -1e29

LANES = 128
SUBLANES = 8
MXU_DIM = 256
VMEM_LIMIT = 56 * 1024 * 1024

ROW_TILE = 256
NSA_Q_TILE = 128
FLASH_TILE = 512
FLASH_SCORE_BYTES = 4 * 1024 * 1024
CUMSUM_TILE = 512
PAGE = 128


def _cparams(sem):
    return pltpu.CompilerParams(dimension_semantics=sem, vmem_limit_bytes=VMEM_LIMIT)


def _resident(shape):
    zeros = (0,) * len(shape)
    return pl.BlockSpec(shape, lambda *_: zeros)


def _dot(a, b):
    return jnp.dot(a, b, preferred_element_type=F32)


def _dot_nt(a, b):
    return lax.dot_general(a, b, (((1,), (1,)), ((), ())), preferred_element_type=F32)


def _split3(x):
    hi = x.astype(BF16)
    r = x - hi.astype(F32)
    mid = r.astype(BF16)
    lo = (r - mid.astype(F32)).astype(BF16)
    return hi, mid, lo


def _split2(x):
    hi = x.astype(BF16)
    return hi, (x - hi.astype(F32)).astype(BF16)


def _split3_trunc(x):
    def top(v):
        bits = lax.bitcast_convert_type(v, jnp.uint32) & jnp.uint32(0xFFFF0000)
        return lax.bitcast_convert_type(bits, F32)
    hi = top(x)
    mid = top(x - hi)
    lo = top(x - hi - mid)
    return hi.astype(BF16), mid.astype(BF16), lo.astype(BF16)


def _masked_softmax(s, valid):
    s = jnp.where(valid, s, NEG_BIG)
    m = jnp.max(s, axis=-1, keepdims=True)
    p = jnp.where(valid, jnp.exp(s - m), 0.0)
    return p * (1.0 / jnp.maximum(jnp.sum(p, axis=-1, keepdims=True), 1e-30))


def _t5_thresholds():
    n = np.arange(0, 4 * MAX_DISTANCE)
    exact = N_BUCKETS // 2
    nf = np.maximum(n, 1).astype(np.float32)
    large = exact + (np.log(nf / np.float32(exact)) / np.float32(math.log(MAX_DISTANCE / exact))
                     * np.float32(N_BUCKETS - exact)).astype(np.int32)
    b = np.where(n < exact, n, np.minimum(large, N_BUCKETS - 1))
    assert np.all(np.diff(b) >= 0) and b[-1] == N_BUCKETS - 1
    return tuple(int(np.argmax(b >= k)) for k in range(1, N_BUCKETS))


_T5_THR = _t5_thresholds()


def _t5_kernel(dist_ref, tab_ref, out_ref):
    d = dist_ref[...]
    n = jnp.maximum(d, 0)
    tab = tab_ref[...]
    val = jnp.broadcast_to(tab[:, 0:1], d.shape)
    for k in range(1, N_BUCKETS):
        val = jnp.where(n >= _T5_THR[k - 1], tab[:, k:k + 1], val)
    out_ref[...] = jnp.where(d < 0, NEG_BIG, val - tab[:, N_BUCKETS - 1:N_BUCKETS])


def _t5_bias(dist, head_of_row, table):
    rows, cols = dist.shape
    blk = math.gcd(rows, 256)
    tab_rows = jnp.transpose(table)[jnp.asarray(head_of_row, jnp.int32)]
    return pl.pallas_call(
        _t5_kernel,
        grid=(rows // blk,),
        in_specs=[pl.BlockSpec((blk, cols), lambda i: (i, 0)),
                  pl.BlockSpec((blk, N_BUCKETS), lambda i: (i, 0))],
        out_specs=pl.BlockSpec((blk, cols), lambda i: (i, 0)),
        out_shape=jax.ShapeDtypeStruct((rows, cols), F32),
        compiler_params=_cparams(("arbitrary",)),
        name="t5_bias",
    )(jnp.asarray(dist, jnp.int32), tab_rows.astype(F32))


def _proj_layout(d_model):
    nsa_h = (d_model // 2) // HEAD_DIM
    splits = (nsa_h * HEAD_DIM, 6 * NSA_KVH * HEAD_DIM, 3 * nsa_h,
              nsa_h * HEAD_DIM, 2 * FOX_KVH * HEAD_DIM, nsa_h,
              (nsa_h // 2) * 2 * HEAD_DIM, 2 * DIFF_KVH * 2 * HEAD_DIM, N_BRANCH * d_model)
    off = np.concatenate([[0], np.cumsum(splits)])
    nq, nkv, ng, fq, fkv, ff, dq, dkv, mg = [np.arange(off[i], off[i + 1]) for i in range(9)]
    small_pad = LANES - len(ng) - len(ff)
    cols = np.concatenate([nq, nkv, fq, fkv, dq, dkv, mg, ng, ff])
    return cols, small_pad, int(off[-1])


def _proj_kernel(x_ref, g_ref, w_ref, gain_ref, bf_ref, m_ref,
                 nq_ref, cmp_ref, sel_ref, win_ref, fq_ref, fox_ref, dq_ref, diff_ref, mg_ref, small_ref,
                 *, d_model):
    bw = d_model // 2
    x = x_ref[...]
    xn = x * lax.rsqrt(jnp.mean(x * x, axis=-1, keepdims=True) + RMS_EPS) * g_ref[...]
    xb = xn.astype(BF16)

    def seg(a, b):
        return _dot(xb, w_ref[:, a:b])

    def head_norm(p, a):
        outs = []
        wdt = p.shape[1]
        step = min(wdt, MXU_DIM)
        for c in range(0, wdt, step):
            pc = p[:, c:c + step]
            ms = _dot((pc * pc).astype(BF16), m_ref[:step, :step]) * (1.0 / HEAD_DIM)
            outs.append(pc * lax.rsqrt(ms + RMS_EPS) * gain_ref[:, a + c:a + c + step])
        return outs[0] if len(outs) == 1 else jnp.concatenate(outs, axis=1)

    scale = HEAD_DIM ** -0.5
    o = 0
    nq_ref[...] = (head_norm(seg(o, o + bw), o) * scale).astype(BF16)
    o += bw
    kvw = NSA_KVH * HEAD_DIM
    for ref in (cmp_ref, sel_ref, win_ref):
        p = seg(o, o + 2 * kvw)
        ref[...] = jnp.concatenate([head_norm(p[:, :kvw], o), p[:, kvw:]], axis=1)
        o += 2 * kvw
    fq_ref[...] = (head_norm(seg(o, o + bw), o) * scale).astype(BF16)
    o += bw
    fkw = FOX_KVH * HEAD_DIM
    p = seg(o, o + 2 * fkw)
    fox_ref[...] = jnp.concatenate([head_norm(p[:, :fkw], o), p[:, fkw:]], axis=1)
    o += 2 * fkw
    dq_ref[...] = (head_norm(seg(o, o + bw), o) * scale).astype(BF16)
    o += bw
    dkw = DIFF_KVH * 2 * HEAD_DIM
    p = seg(o, o + 2 * dkw)
    diff_ref[...] = jnp.concatenate([head_norm(p[:, :dkw], o), p[:, dkw:]], axis=1)
    o += 2 * dkw
    chunk = 2 * MXU_DIM
    for c in range(0, N_BRANCH * d_model, chunk):
        mg_ref[:, c:c + chunk] = jax.nn.sigmoid(seg(o + c, o + c + chunk))
    o += N_BRANCH * d_model
    p = seg(o, o + LANES)
    n_gate = 3 * (bw // HEAD_DIM)
    lane = lax.broadcasted_iota(jnp.int32, p.shape, 1)
    z = p + bf_ref[...]
    logf = jnp.minimum(z, 0.0) - jnp.log1p(jnp.exp(-jnp.abs(z)))
    small_ref[...] = jnp.where(lane < n_gate, jax.nn.sigmoid(p),
                               jnp.where(lane < n_gate + FORGET_GROUPS, logf, 0.0))


def _project(x2d, g_norm, w_perm, gain_row, bf_row, m_blk):
    n, d_model = x2d.shape
    bw = d_model // 2
    wtot = w_perm.shape[1]
    tm = min(ROW_TILE, n)
    kvw = 2 * NSA_KVH * HEAD_DIM
    widths = [(bw, BF16), (kvw, F32), (kvw, F32), (kvw, F32), (bw, BF16), (2 * FOX_KVH * HEAD_DIM, F32),
              (bw, BF16), (2 * DIFF_KVH * 2 * HEAD_DIM, F32), (N_BRANCH * d_model, F32), (LANES, F32)]
    row = lambda w: pl.BlockSpec((tm, w), lambda i: (i, 0))
    return pl.pallas_call(
        functools.partial(_proj_kernel, d_model=d_model),
        grid=(n // tm,),
        in_specs=[row(d_model), _resident((1, d_model)), _resident((d_model, wtot)), _resident((1, wtot)),
                  _resident((1, LANES)), _resident((MXU_DIM, MXU_DIM))],
        out_specs=[row(w) for w, _ in widths],
        out_shape=[jax.ShapeDtypeStruct((n, w), dt) for w, dt in widths],
        compiler_params=_cparams(("arbitrary",)),
        name="proj",
    )(x2d, g_norm, w_perm, gain_row, bf_row, m_blk)


def _cumsum_kernel(x_ref, tri_ref, out_ref, carry_ref):
    @pl.when(pl.program_id(1) == 0)
    def _():
        carry_ref[...] = jnp.zeros_like(carry_ref)

    tri = tri_ref[...]
    hi, mid, lo = _split3(x_ref[0])
    cs = _dot(tri, hi) + _dot(tri, mid) + _dot(tri, lo) + carry_ref[...]
    out_ref[0] = cs
    carry_ref[...] = cs[cs.shape[0] - 1:, :]


def _cumsum_rows(x):
    b, s, w = x.shape
    t = min(CUMSUM_TILE, s)
    tri = jnp.asarray(np.tril(np.ones((t, t), np.float32)), BF16)
    return pl.pallas_call(
        _cumsum_kernel,
        grid=(b, s // t),
        in_specs=[pl.BlockSpec((1, t, w), lambda i, j: (i, j, 0)), _resident((t, t))],
        out_specs=pl.BlockSpec((1, t, w), lambda i, j: (i, j, 0)),
        out_shape=jax.ShapeDtypeStruct((b, s, w), F32),
        scratch_shapes=[pltpu.VMEM((1, w), F32)],
        compiler_params=_cparams(("arbitrary", "arbitrary")),
        name="cumsum",
    )(x, tri)


def _compress_tokens(x, pe1, pe2, w1, w2):
    first = _dot((x + pe1).astype(BF16), w1)
    second = _dot((x + pe2).astype(BF16), w2)
    n = x.shape[0]
    return first + pltpu.roll(second, shift=n - 1, axis=0)


def _compress_prompt_kernel(x_ref, pe1_ref, pe2_ref, w1_ref, w2_ref, out_ref, *, front):
    tok = _compress_tokens(x_ref[0], pe1_ref[...], pe2_ref[...], w1_ref[...], w2_ref[...])
    n, w = tok.shape
    out_ref[0, 0:front, :] = jnp.zeros((front, w), F32)
    out_ref[0, front:front + n, :] = tok
    out_ref[0, front + n:, :] = jnp.zeros((SUBLANES, w), F32)


def _compress_prompt(x, pe1, pe2, w1, w2):
    b, n, cw = x.shape
    w = w1.shape[1]
    front = n - SUBLANES
    return pl.pallas_call(
        functools.partial(_compress_prompt_kernel, front=front),
        grid=(b,),
        in_specs=[pl.BlockSpec((1, n, cw), lambda i: (i, 0, 0)), _resident((1, cw)), _resident((1, cw)),
                  _resident((cw, w)), _resident((cw, w))],
        out_specs=pl.BlockSpec((1, 2 * n, w), lambda i: (i, 0, 0)),
        out_shape=jax.ShapeDtypeStruct((b, 2 * n, w), F32),
        compiler_params=_cparams(("arbitrary",)),
        name="compress_prompt",
    )(x, pe1, pe2, w1, w2)


def _compress_sample_kernel(pt_ref, *refs, n_pages):
    del pt_ref
    page_refs = refs[:n_pages]
    new_ref, pe1_ref, pe2_ref, w1_ref, w2_ref, out_ref, rows_ref = refs[n_pages:]
    chunks = []
    for r in page_refs:
        n_half = rows_ref.shape[0]
        for h in range(n_half):
            rows_ref[h] = jnp.transpose(r[0, 0, h * LANES:(h + 1) * LANES, :])
        chunks.append(jnp.concatenate([rows_ref[h, pl.ds(l, PAGE // D_CMP, stride=D_CMP), :]
                                       for l in range(D_CMP) for h in range(n_half)], axis=1))
    x = jnp.concatenate(chunks + [new_ref[0]], axis=0)
    tok = _compress_tokens(x, pe1_ref[...], pe2_ref[...], w1_ref[...], w2_ref[...])
    out_ref[0] = tok[:out_ref.shape[1], :]


def _compress_sample(pool_t, layer, page_table, new_chunk, pe1, pe2, w1, w2):
    bs, n_pages = page_table.shape
    feat = pool_t.shape[2]
    cw = new_chunk.shape[2]
    w = w1.shape[1]
    n_tok = n_pages * (PAGE // D_CMP)
    page_specs = [pl.BlockSpec((1, 1, feat, PAGE), lambda b, pt, j=j: (layer, pt[b, j], 0, 0))
                  for j in range(n_pages)]
    const = lambda shape: pl.BlockSpec(shape, lambda b, pt: (0,) * len(shape))
    return pl.pallas_call(
        functools.partial(_compress_sample_kernel, n_pages=n_pages),
        grid_spec=pltpu.PrefetchScalarGridSpec(
            num_scalar_prefetch=1, grid=(bs,),
            in_specs=page_specs + [pl.BlockSpec((1, SUBLANES, cw), lambda b, pt: (b, 0, 0)),
                                   const((1, cw)), const((1, cw)), const((cw, w)), const((cw, w))],
            out_specs=pl.BlockSpec((1, n_tok, w), lambda b, pt: (b, 0, 0)),
            scratch_shapes=[pltpu.VMEM((feat // LANES, PAGE, LANES), F32)]),
        out_shape=jax.ShapeDtypeStruct((bs, n_tok, w), F32),
        compiler_params=_cparams(("arbitrary",)),
        name="compress_sample",
    )(page_table, *([pool_t] * n_pages), new_chunk, pe1, pe2, w1, w2)


def _select_top_blocks(score_t, n_top):
    nb = score_t.shape[0]
    n_f = lax.broadcasted_iota(jnp.int32, score_t.shape, 0).astype(F32)
    taken = jnp.zeros(score_t.shape, F32)
    for _ in range(n_top):
        free = taken < 0.5
        cand = jnp.where(free, score_t, -jnp.inf)
        m = jnp.max(cand, axis=0, keepdims=True)
        idx = jnp.min(jnp.where(free & (cand == m), n_f, float(nb)), axis=0, keepdims=True)
        taken = jnp.where(n_f == idx, 1.0, taken)
    return taken


def _nsa_local_kernel(q_ref, tok_ref, cb_ref, win_ref, wb_ref, small_ref, out_ref, sel_ref, *, n_chunk, n_top):
    tq = NSA_Q_TILE
    i = pl.program_id(1)
    t0 = i * tq
    n_heads = cb_ref.shape[0]
    rep = n_heads // NSA_KVH
    q = q_ref[0]
    gates = small_ref[0]
    tok = tok_ref[0, pl.ds(pl.multiple_of(i * (tq // D_CMP), SUBLANES), n_chunk), :]
    wrows = win_ref[0, pl.ds(pl.multiple_of(t0, tq), tq + WINDOW), :]
    kvw = NSA_KVH * HEAD_DIM
    front = n_chunk - SUBLANES
    w_io = lax.broadcasted_iota(jnp.int32, (tq, n_chunk), 1)
    c_exists = w_io >= front - i * (tq // D_CMP)
    lw_io = lax.broadcasted_iota(jnp.int32, (tq, tq + WINDOW), 1)
    w_exists = lw_io >= WINDOW - t0
    n_io = lax.broadcasted_iota(jnp.int32, (LANES, n_chunk), 0)
    c_io = lax.broadcasted_iota(jnp.int32, (LANES, n_chunk), 1) + (i * (tq // D_CMP) - front)
    ov_t = jnp.where((c_io * D_CMP < (n_io + 1) * L_SEL) & (c_io * D_CMP + L_CMP > n_io * L_SEL), 1.0, 0.0).astype(BF16)
    blk = lax.broadcasted_iota(jnp.int32, (LANES, tq), 0)
    cur = 2 * i + (lax.broadcasted_iota(jnp.int32, (LANES, tq), 1) >= L_SEL).astype(jnp.int32)
    c_valid = (cb_ref[0] > MASKED_BELOW) & c_exists
    w_valid = (wb_ref[0] > MASKED_BELOW) & w_exists
    for g in range(NSA_KVH):
        kc = tok[:, g * HEAD_DIM:(g + 1) * HEAD_DIM].astype(BF16)
        vc = tok[:, kvw + g * HEAD_DIM:kvw + (g + 1) * HEAD_DIM].astype(BF16)
        kw = wrows[:, g * HEAD_DIM:(g + 1) * HEAD_DIM]
        vw = wrows[:, kvw + g * HEAD_DIM:kvw + (g + 1) * HEAD_DIM]
        psum = jnp.zeros((tq, n_chunk), F32)
        for r in range(rep):
            h = g * rep + r
            qh = q[:, h * HEAD_DIM:(h + 1) * HEAD_DIM]
            p_c = _masked_softmax(_dot_nt(qh, kc) + cb_ref[h], c_valid)
            o_c = _dot(p_c.astype(BF16), vc)
            psum = psum + p_c
            p_w = _masked_softmax(_dot_nt(qh, kw) + wb_ref[h], w_valid)
            o_w = _dot(p_w.astype(BF16), vw)
            out_ref[0, :, h * HEAD_DIM:(h + 1) * HEAD_DIM] = (gates[:, 3 * h:3 * h + 1] * o_c
                                                             + gates[:, 3 * h + 2:3 * h + 3] * o_w)
        p_hi, p_lo = _split2(psum)
        imp_t = _dot_nt(ov_t, p_hi) + _dot_nt(ov_t, p_lo)
        forced = (blk == 0) | (blk == cur) | (blk == cur - 1)
        score = jnp.where(blk > cur, -jnp.inf, jnp.where(forced, jnp.inf, imp_t))
        taken = _select_top_blocks(score, n_top)
        sel_ref[0, g] = jnp.where(jnp.transpose(taken) > 0.5, 0.0, NEG_BIG).astype(BF16)


def _nsa_local(q, tok, cb, win, wb, small):
    b, s, qw = q.shape
    n_chunk = tok.shape[1] // 2
    n_heads = cb.shape[0]
    tq = NSA_Q_TILE
    n_top = min(N_SEL, s // L_SEL)
    return pl.pallas_call(
        functools.partial(_nsa_local_kernel, n_chunk=n_chunk, n_top=n_top),
        grid=(b, s // tq),
        in_specs=[pl.BlockSpec((1, tq, qw), lambda bi, i: (bi, i, 0)),
                  pl.BlockSpec((1,) + tok.shape[1:], lambda bi, i: (bi, 0, 0)),
                  _resident(cb.shape),
                  pl.BlockSpec((1,) + win.shape[1:], lambda bi, i: (bi, 0, 0)),
                  _resident(wb.shape),
                  pl.BlockSpec((1, tq, LANES), lambda bi, i: (bi, i, 0))],
        out_specs=[pl.BlockSpec((1, tq, qw), lambda bi, i: (bi, i, 0)),
                   pl.BlockSpec((1, NSA_KVH, tq, LANES), lambda bi, i: (bi, 0, i, 0))],
        out_shape=[jax.ShapeDtypeStruct((b, s, qw), F32),
                   jax.ShapeDtypeStruct((b, NSA_KVH, s, LANES), BF16)],
        compiler_params=_cparams(("arbitrary", "arbitrary")),
        name="nsa_local",
    )(q, tok, cb, win, wb, small)


def _flash_kernel(*refs, kind, n_units, dv, n_near, lam_init, gate_base, far_width, stack):
    if kind == "diff":
        q_ref, k_ref, v_ref, near_ref, lam_ref, subln_ref, out_ref, m_ref, acc_ref = refs
    elif kind == "sel":
        q_ref, k_ref, v_ref, near_ref, small_ref, out_ref, m_ref, acc_ref = refs
    else:
        q_ref, k_ref, v_ref, near_ref, out_ref, m_ref, acc_ref = refs
    t = FLASH_TILE
    g = pl.program_id(1)
    i = pl.program_id(2)
    m_ref[...] = jnp.full(m_ref.shape, NEG_BIG, F32)
    acc_ref[...] = jnp.zeros(acc_ref.shape, F32)
    def step(j, near_idx, width):
        start = pl.multiple_of(j * t, t)
        kt = k_ref[0, 0, pl.ds(start, width * t), :]
        vt = v_ref[0, 0, pl.ds(start, width * t), :]
        for u0 in range(0, n_units, stack):
            rows = stack * t
            r0 = u0 * t
            s = _dot_nt(q_ref[0, 0, u0:u0 + stack].reshape(rows, q_ref.shape[-1]), kt)
            if near_idx is not None:
                if near_ref.shape[1] == n_units:
                    near = near_ref[0, u0:u0 + stack, near_idx]
                else:
                    near = jnp.broadcast_to(near_ref[0, :, near_idx], (stack, t, t))
                s = s + near.reshape(rows, t)
            m_old = m_ref[r0:r0 + rows, :]
            m_new = jnp.maximum(m_old, jnp.max(s, axis=-1, keepdims=True))
            p = jnp.exp(s - m_new)
            acc_ref[r0:r0 + rows, :] = (acc_ref[r0:r0 + rows, :] * jnp.exp(m_old - m_new)
                                        + _dot(p.astype(BF16), vt))
            m_ref[r0:r0 + rows, :] = m_new

    n_far = jnp.maximum(i - (n_near - 1), 0)
    if far_width == 2:
        def far_pair(jj, carry):
            step(2 * jj, None, 2)
            return carry

        lax.fori_loop(0, n_far // 2, far_pair, 0)

        @pl.when(n_far % 2 == 1)
        def _():
            step(n_far - 1, None, 1)
    else:
        def far_body(j, carry):
            step(j, None, 1)
            return carry

        lax.fori_loop(0, n_far, far_body, 0)
    if n_near == 2:
        @pl.when(i >= 1)
        def _():
            step(i - 1, 1, 1)
    step(i, 0, 1)

    def unit_out(u):
        acc = acc_ref[u * t:(u + 1) * t, :]
        return acc[:, :dv] / acc[:, dv:dv + 1]

    if kind == "diff":
        lp = lam_ref[...]
        lam = (jnp.exp(jnp.sum(lp[0:1] * lp[1:2], axis=-1, keepdims=True))
               - jnp.exp(jnp.sum(lp[2:3] * lp[3:4], axis=-1, keepdims=True)) + lam_init)
        for r in range(n_units // 2):
            o = unit_out(2 * r) - lam * unit_out(2 * r + 1)
            y = o * lax.rsqrt(jnp.mean(o * o, axis=-1, keepdims=True) + RMS_EPS) * subln_ref[...]
            out_ref[0, :, r * dv:(r + 1) * dv] = y * (1.0 - lam_init)
    elif kind == "sel":
        gates = small_ref[0]
        for u in range(n_units):
            col = 3 * (g * n_units + u) + gate_base
            lane = lax.broadcasted_iota(jnp.int32, gates.shape, 1)
            gate = jnp.sum(jnp.where(lane == col, gates, 0.0), axis=-1, keepdims=True)
            out_ref[0, :, u * dv:(u + 1) * dv] = gate * unit_out(u)
    else:
        for u in range(n_units):
            out_ref[0, :, u * dv:(u + 1) * dv] = unit_out(u)


def _flash(kind, q, k, v, near, extras, *, dv, out_width, lam_init=0.0):
    b, n_g, n_units, s, dk = q.shape
    dva = v.shape[-1]
    t = FLASH_TILE
    n_near = near.shape[2]
    wg = out_width // n_g
    in_specs = [pl.BlockSpec((1, 1, n_units, t, dk), lambda bi, g, i: (bi, g, 0, i, 0)),
                pl.BlockSpec((1, 1, s, dk), lambda bi, g, i: (bi, g, 0, 0)),
                pl.BlockSpec((1, 1, s, dva), lambda bi, g, i: (bi, g, 0, 0))]
    if near.shape[0] > 1:
        in_specs.append(pl.BlockSpec((1,) + near.shape[1:], lambda bi, g, i: (g, 0, 0, 0, 0)))
    else:
        in_specs.append(_resident(near.shape))
    if kind == "diff":
        in_specs += [_resident(extras[0].shape), _resident(extras[1].shape)]
    elif kind == "sel":
        in_specs.append(pl.BlockSpec((1, t, LANES), lambda bi, g, i: (bi, i, 0)))
    stack = 2 if n_units % 2 == 0 and dva <= LANES else 1
    far_width = 2 if stack * t * 2 * t * 4 <= FLASH_SCORE_BYTES else 1
    return pl.pallas_call(
        functools.partial(_flash_kernel, kind=kind, n_units=n_units, dv=dv, n_near=n_near, lam_init=lam_init,
                          gate_base=1, far_width=far_width, stack=stack),
        grid=(b, n_g, s // t),
        in_specs=in_specs,
        out_specs=pl.BlockSpec((1, t, wg), lambda bi, g, i: (bi, i, g)),
        out_shape=jax.ShapeDtypeStruct((b, s, out_width), F32),
        scratch_shapes=[pltpu.VMEM((n_units * t, 1), F32), pltpu.VMEM((n_units * t, dva), F32)],
        compiler_params=_cparams(("arbitrary", "arbitrary", "arbitrary")),
        name="flash_" + kind,
    )(q, k, v, near, *extras)


def _merge_kernel(*refs, n_nsa):
    nsa_refs = refs[:n_nsa]
    fox_ref, diff_ref, mg_ref, x_ref, wbr_ref, wo_ref, out_ref = refs[n_nsa:]
    d = x_ref.shape[1]
    o_a = nsa_refs[0][...]
    for r in nsa_refs[1:]:
        o_a = o_a + r[...]
    mix = jnp.zeros(x_ref.shape, F32)
    for n, o in enumerate((o_a, fox_ref[...], diff_ref[...])):
        mix = mix + mg_ref[:, n * d:(n + 1) * d] * _dot(o.astype(BF16), wbr_ref[n])
    out_ref[...] = x_ref[...] + _dot(mix.astype(BF16), wo_ref[...])


def _merge(nsa_parts, o_fox, o_diff, mg, x2d, w_br, w_o):
    n, d = x2d.shape
    bw = o_fox.shape[1]
    tm = min(ROW_TILE, n)
    row = lambda w: pl.BlockSpec((tm, w), lambda i: (i, 0))
    return pl.pallas_call(
        functools.partial(_merge_kernel, n_nsa=len(nsa_parts)),
        grid=(n // tm,),
        in_specs=[row(bw)] * (len(nsa_parts) + 2) + [row(N_BRANCH * d), row(d), _resident(w_br.shape),
                                                     _resident(w_o.shape)],
        out_specs=row(d),
        out_shape=jax.ShapeDtypeStruct((n, d), F32),
        compiler_params=_cparams(("arbitrary",)),
        name="merge",
    )(*nsa_parts, o_fox, o_diff, mg, x2d, w_br, w_o)


def _ffn_kernel(x_ref, g_ref, wgu_ref, wd_ref, out_ref, *, d_ff):
    x = x_ref[...]
    h = (x * lax.rsqrt(jnp.mean(x * x, axis=-1, keepdims=True) + RMS_EPS) * g_ref[...]).astype(BF16)
    acc = x
    for c in range(0, d_ff, MXU_DIM):
        gate = _dot(h, wgu_ref[:, c:c + MXU_DIM])
        up = _dot(h, wgu_ref[:, d_ff + c:d_ff + c + MXU_DIM])
        acc = acc + _dot((jax.nn.silu(gate) * up).astype(BF16), wd_ref[c:c + MXU_DIM, :])
    out_ref[...] = acc


def _ffn(x2d, g_norm, w_gu, w_d):
    n, d = x2d.shape
    d_ff = w_d.shape[0]
    tm = min(ROW_TILE, n)
    row = pl.BlockSpec((tm, d), lambda i: (i, 0))
    return pl.pallas_call(
        functools.partial(_ffn_kernel, d_ff=d_ff),
        grid=(n // tm,),
        in_specs=[row, _resident((1, d)), _resident(w_gu.shape), _resident(w_d.shape)],
        out_specs=row,
        out_shape=jax.ShapeDtypeStruct((n, d), F32),
        compiler_params=_cparams(("arbitrary",)),
        name="ffn",
    )(x2d, g_norm, w_gu, w_d)


def _fold_groups(o, group_of_row, n_groups, width):
    out = jnp.zeros((o.shape[0], width), F32)
    for gi in range(n_groups):
        out = out + jnp.where(group_of_row == gi, o[:, gi * width:(gi + 1) * width], 0.0)
    return out


def _row_group(rows, period, per_group):
    r = lax.broadcasted_iota(jnp.int32, (rows, 1), 0)
    return (r % period) // per_group


def _samp_cmp_kernel(q_ref, tok_ref, cb_ref, ov_ref, rsum_ref, cur_ref, expand_ref, t5_ref, gate_ref,
                     out_ref, bias_ref, *, n_top):
    q = q_ref[0]
    tok = tok_ref[0]
    kvw = NSA_KVH * HEAD_DIM
    cb = cb_ref[...]
    p = _masked_softmax(_dot_nt(q, tok[:, :kvw].astype(BF16)) + cb, cb > MASKED_BELOW)
    o = _dot(p.astype(BF16), tok[:, kvw:].astype(BF16))
    rows = q.shape[0]
    rep = rows // (SUBLANES * NSA_KVH)
    grp = _row_group(rows, NSA_KVH * rep, rep)
    out_ref[0] = gate_ref[0][:, 0:1] * _fold_groups(o, grp, NSA_KVH, HEAD_DIM)
    p_hi, p_lo = _split2(p)
    psum = _dot(rsum_ref[...], p_hi) + _dot(rsum_ref[...], p_lo)
    s_hi, s_lo = _split2(psum)
    imp = _dot(s_hi, ov_ref[...]) + _dot(s_lo, ov_ref[...])
    blk = lax.broadcasted_iota(jnp.int32, imp.shape, 1)
    cur = cur_ref[...]
    forced = (blk == 0) | (blk == cur) | (blk == cur - 1)
    score = jnp.where(blk > cur, -jnp.inf, jnp.where(forced, jnp.inf, imp))
    n_f = blk.astype(F32)
    taken = jnp.zeros(imp.shape, F32)
    for _ in range(n_top):
        free = taken < 0.5
        cand = jnp.where(free, score, -jnp.inf)
        m = jnp.max(cand, axis=-1, keepdims=True)
        idx = jnp.min(jnp.where(free & (cand == m), n_f, float(imp.shape[1])), axis=-1, keepdims=True)
        taken = jnp.where(n_f == idx, 1.0, taken)
    key_sel = _dot(taken.astype(BF16), expand_ref[...])
    bias_ref[0] = jnp.where(key_sel > 0.5, t5_ref[...], NEG_BIG)


def _samp_cmp(qbd, tok, cb, ov, rsum, cur, expand, t5, gate, n_top):
    bs, rows, dk = qbd.shape
    lp = expand.shape[1]
    per_b = lambda shape: pl.BlockSpec((1,) + shape[1:], lambda b: (b,) + (0,) * (len(shape) - 1))
    return pl.pallas_call(
        functools.partial(_samp_cmp_kernel, n_top=n_top),
        grid=(bs,),
        in_specs=[per_b(qbd.shape), per_b(tok.shape), _resident(cb.shape), _resident(ov.shape), _resident(rsum.shape),
                  _resident(cur.shape), _resident(expand.shape), _resident(t5.shape), per_b(gate.shape)],
        out_specs=[pl.BlockSpec((1, rows, HEAD_DIM), lambda b: (b, 0, 0)),
                   pl.BlockSpec((1, rows, lp), lambda b: (b, 0, 0))],
        out_shape=[jax.ShapeDtypeStruct((bs, rows, HEAD_DIM), F32), jax.ShapeDtypeStruct((bs, rows, lp), F32)],
        compiler_params=_cparams(("arbitrary",)),
        name="samp_cmp",
    )(qbd, tok, cb, ov, rsum, cur, expand, t5, gate)


def _samp_fox_bias_kernel(pt_ref, *refs, n_pages, n_new):
    del pt_ref
    page_refs = refs[:n_pages]
    new_ref, tri_ref, out_ref = refs[n_pages:]
    tri = tri_ref[...]

    def lane_cumsum(x):
        hi, mid, lo = _split3(x)
        return _dot(hi, tri) + _dot(mid, tri) + _dot(lo, tri)

    carry = jnp.zeros((FORGET_GROUPS, 1), F32)
    c_tiles = []
    for r in page_refs:
        c = lane_cumsum(r[0, 0]) + carry
        carry = c[:, PAGE - 1:PAGE]
        c_tiles.append(c)
    c_new = lane_cumsum(new_ref[0]) + carry
    lane = lax.broadcasted_iota(jnp.int32, (FORGET_GROUPS, PAGE), 1)
    for t in range(n_new):
        c_q = c_new[:, t:t + 1]
        lo_r, hi_r = t * FORGET_GROUPS, (t + 1) * FORGET_GROUPS
        for j, c in enumerate(c_tiles):
            out_ref[0, lo_r:hi_r, j * PAGE:(j + 1) * PAGE] = c_q - c
        out_ref[0, lo_r:hi_r, n_pages * PAGE:] = jnp.where(lane <= t, c_q - c_new, NEG_BIG)


def _samp_fox_bias(pool_t, layer, page_table, logf_new_t):
    bs, n_pages = page_table.shape
    n_new = SUBLANES
    lp = (n_pages + 1) * PAGE
    tri = jnp.asarray(np.triu(np.ones((PAGE, PAGE), np.float32)), BF16)
    page_specs = [pl.BlockSpec((1, 1, FORGET_GROUPS, PAGE), lambda b, pt, j=j: (layer, pt[b, j], 0, 0))
                  for j in range(n_pages)]
    return pl.pallas_call(
        functools.partial(_samp_fox_bias_kernel, n_pages=n_pages, n_new=n_new),
        grid_spec=pltpu.PrefetchScalarGridSpec(
            num_scalar_prefetch=1, grid=(bs,),
            in_specs=page_specs + [pl.BlockSpec((1, FORGET_GROUPS, PAGE), lambda b, pt: (b, 0, 0)),
                                   pl.BlockSpec((PAGE, PAGE), lambda b, pt: (0, 0))],
            out_specs=pl.BlockSpec((1, n_new * FORGET_GROUPS, lp), lambda b, pt: (b, 0, 0))),
        out_shape=jax.ShapeDtypeStruct((bs, n_new * FORGET_GROUPS, lp), F32),
        compiler_params=_cparams(("arbitrary",)),
        name="samp_fox_bias",
    )(page_table, *([pool_t] * n_pages), logf_new_t, tri)


def _samp_attn_kernel(*refs, kind, paged, layout, n_kv, dk, dv, n_groups, lam_init):
    if paged:
        refs = refs[1:]
    q_ref = refs[0]
    kv_refs = refs[1:1 + n_kv]
    rest = refs[1 + n_kv:]
    if kind == "diff":
        new_ref, bias_ref, lam_ref, subln_ref, out_ref = rest
    elif kind == "fox":
        new_ref, bias_ref, out_ref = rest
    else:
        new_ref, bias_ref, gate_ref, out_ref = rest
    q = q_ref[0]
    wv = n_groups * dv
    tiles = []
    for r in kv_refs:
        if layout == "transposed":
            tiles.append((r[0, 0, 0:dk, :].astype(BF16), r[0, 0, dk:dk + wv, :].astype(BF16), True))
        elif layout == "interleaved":
            sl = lambda o: [r[0, 0, pl.ds(o + gi, PAGE, stride=2 * n_groups), :] for gi in range(n_groups)]
            tiles.append((jnp.concatenate(sl(0), axis=1).astype(BF16),
                          jnp.concatenate(sl(n_groups), axis=1).astype(BF16), False))
        else:
            for c in range(r.shape[1] // PAGE):
                x = r[0, c * PAGE:(c + 1) * PAGE, :]
                tiles.append((x[:, :dk].astype(BF16), x[:, dk:dk + wv].astype(BF16), False))
    new = new_ref[0]
    new = jnp.concatenate([new, jnp.zeros((PAGE - new.shape[0], new.shape[1]), F32)], axis=0)
    tiles.append((new[:, :dk].astype(BF16), new[:, dk:dk + wv].astype(BF16), False))
    scores = [(_dot(q, k) if tr else _dot_nt(q, k)) + bias_ref[0, :, j * PAGE:(j + 1) * PAGE]
              for j, (k, _, tr) in enumerate(tiles)]
    m = functools.reduce(jnp.maximum, [jnp.max(s, axis=-1, keepdims=True) for s in scores])
    rows = q.shape[0]
    den = jnp.zeros((rows, 1), F32)
    acc = jnp.zeros((rows, wv), F32)
    for s, (_, v, tr) in zip(scores, tiles):
        p = jnp.exp(s - m) * (s > MASKED_BELOW).astype(F32)
        den = den + jnp.sum(p, axis=-1, keepdims=True)
        pb = p.astype(BF16)
        acc = acc + (_dot_nt(pb, v) if tr else _dot(pb, v))
    o = acc / jnp.maximum(den, 1e-30)
    if kind == "diff":
        lp = lam_ref[...]
        lam = (jnp.exp(jnp.sum(lp[0:1] * lp[1:2], axis=-1, keepdims=True))
               - jnp.exp(jnp.sum(lp[2:3] * lp[3:4], axis=-1, keepdims=True)) + lam_init)
        half = rows // 2
        a = o[:half] - lam * o[half:]
        rep = half // (SUBLANES * n_groups)
        a = _fold_groups(a, _row_group(half, n_groups * rep, rep), n_groups, dv)
        y = a * lax.rsqrt(jnp.mean(a * a, axis=-1, keepdims=True) + RMS_EPS) * subln_ref[...]
        out_ref[0] = y * (1.0 - lam_init)
    else:
        rep = rows // (SUBLANES * n_groups)
        folded = _fold_groups(o, _row_group(rows, n_groups * rep, rep), n_groups, dv)
        if kind == "fox":
            out_ref[0] = folded
        else:
            col = 1 if kind == "sel" else 2
            out_ref[0] = gate_ref[0][:, col:col + 1] * folded


def _samp_attn(kind, qbd, kv, layer, page_table, new_rows, bias, extras, *, dk, dv, n_groups, layout="rows",
               lam_init=0.0):
    bs, rows, _ = qbd.shape
    paged = page_table is not None
    out_rows = rows // 2 if kind == "diff" else rows
    if paged:
        n_kv = page_table.shape[1]
        im = lambda f: (lambda b, pt: f(b))
        kv_specs = [pl.BlockSpec((1, 1) + kv.shape[2:], lambda b, pt, j=j: (layer, pt[b, j], 0, 0))
                    for j in range(n_kv)]
        kv_args = [kv] * n_kv
    else:
        n_kv = 1
        im = lambda f: (lambda b: f(b))
        kv_specs = [pl.BlockSpec((1,) + kv.shape[1:], im(lambda b: (b, 0, 0)))]
        kv_args = [kv]
    per_b = lambda shape: pl.BlockSpec((1,) + shape[1:], im(lambda b: (b,) + (0,) * (len(shape) - 1)))
    const = lambda shape: pl.BlockSpec(shape, im(lambda b: (0,) * len(shape)))
    bias_spec = per_b(bias.shape) if bias.shape[0] == bs and bias.ndim == 3 and bias.shape[0] > 1 else const(bias.shape)
    in_specs = [per_b(qbd.shape)] + kv_specs + [per_b(new_rows.shape), bias_spec]
    if kind == "diff":
        in_specs += [const(extras[0].shape), const(extras[1].shape)]
    elif kind != "fox":
        in_specs.append(per_b(extras[0].shape))
    out_spec = pl.BlockSpec((1, out_rows, dv), im(lambda b: (b, 0, 0)))
    body = functools.partial(_samp_attn_kernel, kind=kind, paged=paged, layout=layout, n_kv=n_kv, dk=dk, dv=dv,
                             n_groups=n_groups, lam_init=lam_init)
    out_shape = jax.ShapeDtypeStruct((bs, out_rows, dv), F32)
    args = [qbd] + kv_args + [new_rows, bias] + list(extras)
    if paged:
        return pl.pallas_call(
            body,
            grid_spec=pltpu.PrefetchScalarGridSpec(num_scalar_prefetch=1, grid=(bs,), in_specs=in_specs,
                                                   out_specs=out_spec),
            out_shape=out_shape, compiler_params=_cparams(("arbitrary",)), name="samp_" + kind,
        )(page_table, *args)
    return pl.pallas_call(body, grid=(bs,), in_specs=in_specs, out_specs=out_spec, out_shape=out_shape,
                          compiler_params=_cparams(("arbitrary",)), name="samp_" + kind)(*args)


def _toeplitz_near(t, n_near):
    a = np.arange(t)
    return np.stack([(j * t + a[:, None] - a[None, :]) for j in range(n_near)])


def _prompt_tables(s, nsa_h, diff_h):
    tq = NSA_Q_TILE
    n_chunk = s // D_CMP
    front = n_chunk - SUBLANES
    tl = np.arange(tq)[:, None]
    w = np.arange(n_chunk)[None, :]
    d_cmp = tl - (D_CMP * (w - front) + L_CMP - 1)
    lw = np.arange(tq + WINDOW)[None, :]
    d_win = tl + WINDOW - lw
    d_win = np.where(d_win < WINDOW, d_win, -1)
    near = _toeplitz_near(FLASH_TILE, 2)
    return d_cmp, d_win, near


def kernel(x_prompt, x_sample, cache_nsa_cmp_kv, cache_nsa_sel_kv, state_nsa_win_kv, cache_fox_kv, cache_fox_logf,
           cache_diff_kv, page_table, rel_bias_table, norm_mix, norm_ffn, w_in, b_forget, qk_gain, nsa_cmp_w,
           nsa_cmp_pe, diff_lambda, diff_subln, w_branch, w_out, w_gate_up, w_down):
    bp, s, d_model = x_prompt.shape
    bs, t_new, _ = x_sample.shape
    depth = w_in.shape[0]
    bw = d_model // 2
    nsa_h = bw // HEAD_DIM
    fox_h = nsa_h
    diff_h = nsa_h // 2
    nsa_r = nsa_h // NSA_KVH
    fox_r = fox_h // FOX_KVH
    diff_r = diff_h // DIFF_KVH
    n_pages = page_table.shape[1]
    past = n_pages * PAGE
    win_buf = state_nsa_win_kv.shape[2]
    n_pool = cache_fox_kv.shape[1]
    assert t_new == SUBLANES and fox_h == FORGET_GROUPS and past % L_SEL == 0 and win_buf % PAGE == 0
    assert s % FLASH_TILE == 0 and s // L_SEL <= LANES
    kvw = NSA_KVH * HEAD_DIM
    tab_nsa = rel_bias_table[:, :nsa_h]
    tab_diff = rel_bias_table[:, nsa_h:]

    cols, small_pad, d_proj = _proj_layout(d_model)
    w_perm = jnp.pad(jnp.take(w_in, jnp.asarray(cols), axis=2), ((0, 0), (0, 0), (0, small_pad))).astype(BF16)
    ones = lambda n: jnp.ones((depth, n), F32)
    tile_gain = lambda gi, n_heads: jnp.tile(qk_gain[:, gi], (1, n_heads))
    gain_rows = jnp.concatenate(
        [tile_gain(0, nsa_h)]
        + [x for i in range(3) for x in (tile_gain(1 + i, NSA_KVH), ones(kvw))]
        + [tile_gain(4, fox_h), tile_gain(5, FOX_KVH), ones(FOX_KVH * HEAD_DIM),
           tile_gain(6, 2 * diff_h), tile_gain(7, 2 * DIFF_KVH), ones(DIFF_KVH * 2 * HEAD_DIM),
           ones(N_BRANCH * d_model + LANES)], axis=1)
    bf_rows = jnp.pad(b_forget, ((0, 0), (3 * nsa_h, LANES - 3 * nsa_h - fox_h)))
    m_blk = jnp.asarray(np.kron(np.eye(MXU_DIM // HEAD_DIM), np.ones((HEAD_DIM, HEAD_DIM))), BF16)
    w_br = w_branch.astype(BF16)
    w_o = w_out.astype(BF16)
    w_gu = w_gate_up.astype(BF16)
    w_dn = w_down.astype(BF16)
    slot_kv = np.repeat(np.arange(2), NSA_KVH)
    eye_slot = jnp.asarray(np.eye(2 * NSA_KVH), F32)
    w_slot = nsa_cmp_w[:, slot_kv]
    w_big = jnp.einsum('zsldq,st->zlsdtq', w_slot, eye_slot).reshape(depth, L_CMP, 2 * kvw, 2 * kvw)
    w_c1 = w_big[:, :D_CMP].reshape(depth, D_CMP * 2 * kvw, 2 * kvw).astype(BF16)
    w_c2 = w_big[:, D_CMP:].reshape(depth, D_CMP * 2 * kvw, 2 * kvw).astype(BF16)
    pe_slot = jnp.transpose(nsa_cmp_pe[:, slot_kv], (0, 2, 1, 3))
    pe_c1 = pe_slot[:, :D_CMP].reshape(depth, 1, D_CMP * 2 * kvw)
    pe_c2 = pe_slot[:, D_CMP:].reshape(depth, 1, D_CMP * 2 * kvw)

    t = FLASH_TILE
    d_cmp, d_win, d_near = _prompt_tables(s, nsa_h, diff_h)
    n_chunk = s // D_CMP
    rep_rows = lambda d, n: np.tile(d, (n, 1))
    cb_p = _t5_bias(rep_rows(d_cmp, nsa_h), np.repeat(np.arange(nsa_h), NSA_Q_TILE), tab_nsa
                    ).reshape(nsa_h, NSA_Q_TILE, n_chunk)
    wb_p = _t5_bias(rep_rows(d_win, nsa_h), np.repeat(np.arange(nsa_h), NSA_Q_TILE), tab_nsa
                    ).reshape(nsa_h, NSA_Q_TILE, NSA_Q_TILE + WINDOW)
    near2d = d_near.reshape(2 * t, t)
    near_sel = _t5_bias(rep_rows(near2d, nsa_h), np.repeat(np.arange(nsa_h), 2 * t), tab_nsa
                        ).reshape(NSA_KVH, nsa_r, 2, t, t)
    near_diff = _t5_bias(rep_rows(near2d, diff_h), np.repeat(np.arange(diff_h), 2 * t), tab_diff
                         ).reshape(DIFF_KVH, diff_r, 2, t, t)
    near_diff = jnp.repeat(near_diff, 2, axis=1)
    near_fox = jnp.asarray(np.where(d_near[:1] >= 0, 0.0, NEG_BIG), F32).reshape(1, 1, 1, t, t)
    onehot_blk = jnp.asarray(np.arange(s)[:, None] // L_SEL == np.arange(LANES)[None, :], BF16)

    lp_s = past + PAGE
    pos_q = past + np.arange(t_new)
    key_pos = np.arange(lp_s)
    key_ok = key_pos < past + t_new
    rows_n = t_new * nsa_h
    tok_of_row = np.repeat(np.arange(t_new), nsa_h)
    head_of_row = np.tile(np.arange(nsa_h), t_new)
    d_full = pos_q[tok_of_row][:, None] - key_pos[None, :]
    d_full = np.where(key_ok[None, :], d_full, -1)
    t5_sel_s = _t5_bias(d_full, head_of_row, tab_nsa)
    n_tok_s = past // D_CMP
    d_cmp_s = pos_q[tok_of_row][:, None] - (np.arange(n_tok_s) * D_CMP + L_CMP - 1)[None, :]
    cb_s = _t5_bias(d_cmp_s, head_of_row, tab_nsa)
    lw_s = win_buf + PAGE
    wpos = past - win_buf + np.arange(lw_s)
    d_win_s = pos_q[tok_of_row][:, None] - wpos[None, :]
    d_win_s = np.where((d_win_s < WINDOW) & (wpos < past + t_new)[None, :] & (wpos >= 0)[None, :], d_win_s, -1)
    wb_s = _t5_bias(d_win_s, head_of_row, tab_nsa)[None]
    dtok = np.tile(np.repeat(np.arange(t_new), diff_h), 2)
    dhead = np.tile(np.arange(diff_h), 2 * t_new)
    d_diff_s = np.where(key_ok[None, :], pos_q[dtok][:, None] - key_pos[None, :], -1)
    t5_diff_s = _t5_bias(d_diff_s, dhead, tab_diff)[None]
    ns_s = -(-(past + t_new) // L_SEL)
    cidx = np.arange(n_tok_s)[:, None]
    sidx = np.arange(LANES)[None, :]
    ov_s = jnp.asarray((cidx * D_CMP < (sidx + 1) * L_SEL) & (cidx * D_CMP + L_CMP > sidx * L_SEL), BF16)
    rsum_s = jnp.asarray(np.kron(np.eye(t_new * NSA_KVH), np.ones((nsa_r, nsa_r))), BF16)
    cur_s = jnp.asarray(np.broadcast_to((pos_q[tok_of_row] // L_SEL)[:, None], (rows_n, LANES)), jnp.int32)
    expand_s = jnp.asarray((key_pos[None, :] // L_SEL == np.arange(LANES)[:, None]) & key_ok[None, :], BF16)
    n_top_s = min(N_SEL, ns_s)

    eye_n = jnp.asarray(np.eye(NSA_KVH), BF16)
    eye_f = jnp.asarray(np.eye(FOX_KVH), BF16)
    eye_d = jnp.asarray(np.eye(DIFF_KVH * 2), BF16)

    pages_t = lambda c: jnp.transpose(c, (0, 1, 3, 4, 5, 2)).reshape(depth, n_pool, -1, PAGE)
    pool_cmp = pages_t(cache_nsa_cmp_kv)
    pool_sel = pages_t(cache_nsa_sel_kv)
    pool_fox = pages_t(cache_fox_kv)
    pool_diff = cache_diff_kv.reshape(depth, n_pool, PAGE * 2 * DIFF_KVH, 2 * HEAD_DIM)
    pool_logf_t = jnp.transpose(cache_fox_logf, (0, 1, 3, 2))
    win_state = state_nsa_win_kv.reshape(depth, bs, win_buf, 2 * kvw)

    xp = x_prompt.reshape(bp * s, d_model)
    xs = x_sample.reshape(bs * t_new, d_model)
    st = {n: [] for n in ('cmp_p', 'cmp_s', 'sel_p', 'sel_s', 'win_p', 'win_s', 'fkv_p', 'fkv_s',
                          'flf_p', 'flf_s', 'dkv_p', 'dkv_s')}
    n_gate = 3 * nsa_h

    for l in range(depth):
        lam_init = 0.8 - 0.6 * math.exp(-0.3 * l)
        lam_par = diff_lambda[l].astype(F32)
        subln = diff_subln[l].reshape(1, 2 * HEAD_DIM)
        proj_args = (norm_mix[l].reshape(1, d_model), w_perm[l], gain_rows[l].reshape(1, -1),
                     bf_rows[l].reshape(1, LANES), m_blk)

        nq, cmp_r, sel_r, win_r, fq, fox_rw, dq, diff_rw, mg, small = _project(xp, *proj_args)
        b3 = lambda a: a.reshape(bp, s, a.shape[-1])
        small3 = b3(small)
        tok = _compress_prompt(cmp_r.reshape(bp, n_chunk, D_CMP * 2 * kvw), pe_c1[l], pe_c2[l], w_c1[l], w_c2[l])
        win_pad = jnp.pad(b3(win_r).astype(BF16), ((0, 0), (WINDOW, 0), (0, 0)))
        o_cw, sel_mask = _nsa_local(b3(nq), tok, cb_p, win_pad, wb_p, small3)
        q_heads = jnp.transpose(b3(nq).reshape(bp, s, NSA_KVH, nsa_r, HEAD_DIM), (0, 2, 3, 1, 4))
        pad_q = jnp.zeros((bp, NSA_KVH, nsa_r, s, MXU_DIM - HEAD_DIM - LANES), BF16)
        q_sel = jnp.concatenate([q_heads, jnp.broadcast_to(sel_mask[:, :, None], (bp, NSA_KVH, nsa_r, s, LANES)),
                                 pad_q], axis=-1)
        sel_b = b3(sel_r).astype(BF16)
        k_sel = jnp.transpose(sel_b[..., :kvw].reshape(bp, s, NSA_KVH, HEAD_DIM), (0, 2, 1, 3))
        v_sel = jnp.transpose(sel_b[..., kvw:].reshape(bp, s, NSA_KVH, HEAD_DIM), (0, 2, 1, 3))
        k_sel = jnp.concatenate([k_sel, jnp.broadcast_to(onehot_blk, (bp, NSA_KVH, s, LANES)),
                                 jnp.zeros((bp, NSA_KVH, s, MXU_DIM - HEAD_DIM - LANES), BF16)], axis=-1)
        v_ones = lambda v, width: jnp.concatenate(
            [v, jnp.ones(v.shape[:-1] + (1,), BF16), jnp.zeros(v.shape[:-1] + (width - v.shape[-1] - 1,), BF16)], axis=-1)
        o_sel = _flash("sel", q_sel, k_sel, v_ones(v_sel, LANES), near_sel, (small3,), dv=HEAD_DIM, out_width=bw)
        c_all = _cumsum_rows(small3)[..., n_gate:n_gate + fox_h]
        c_g = jnp.transpose(c_all.reshape(bp, s, FOX_KVH, fox_r), (0, 2, 3, 1))
        c_parts = jnp.stack(_split3_trunc(c_g), axis=-1)
        fq_heads = jnp.transpose(b3(fq).reshape(bp, s, FOX_KVH, fox_r, HEAD_DIM), (0, 2, 3, 1, 4))
        eye_r = jnp.asarray(np.eye(fox_r), BF16)
        cq_cols = jnp.einsum('bgrsc,rq->bgrsqc', c_parts, eye_r).reshape(bp, FOX_KVH, fox_r, s, 3 * fox_r)
        one_cols = jnp.broadcast_to(jnp.repeat(eye_r, 3, axis=1)[None, None, :, None, :],
                                    (bp, FOX_KVH, fox_r, s, 3 * fox_r))
        n_aug = 6 * fox_r
        q_fox = jnp.concatenate([fq_heads, cq_cols, one_cols,
                                 jnp.zeros((bp, FOX_KVH, fox_r, s, LANES - HEAD_DIM - n_aug), BF16)], axis=-1)
        fox_b = b3(fox_rw)
        fkw = FOX_KVH * HEAD_DIM
        k_fox = jnp.transpose(fox_b[..., :fkw].astype(BF16).reshape(bp, s, FOX_KVH, HEAD_DIM), (0, 2, 1, 3))
        v_fox = jnp.transpose(fox_b[..., fkw:].astype(BF16).reshape(bp, s, FOX_KVH, HEAD_DIM), (0, 2, 1, 3))
        ck_cols = -jnp.transpose(c_parts, (0, 1, 3, 2, 4)).reshape(bp, FOX_KVH, s, 3 * fox_r)
        k_fox = jnp.concatenate([k_fox, jnp.ones((bp, FOX_KVH, s, 3 * fox_r), BF16), ck_cols,
                                 jnp.zeros((bp, FOX_KVH, s, LANES - HEAD_DIM - n_aug), BF16)], axis=-1)
        o_fox = _flash("fox", q_fox, k_fox, v_ones(v_fox, LANES), near_fox, (), dv=HEAD_DIM, out_width=bw)
        dq_h = jnp.transpose(b3(dq).reshape(bp, s, DIFF_KVH, diff_r, 2, HEAD_DIM), (0, 2, 3, 4, 1, 5))
        q_diff = jnp.einsum('bgrcsd,ce->bgrcsed', dq_h, jnp.asarray(np.eye(2), BF16)
                            ).reshape(bp, DIFF_KVH, diff_r * 2, s, 2 * HEAD_DIM)
        diff_b = b3(diff_rw).astype(BF16)
        dkw = DIFF_KVH * 2 * HEAD_DIM
        k_diff = jnp.transpose(diff_b[..., :dkw].reshape(bp, s, DIFF_KVH, 2 * HEAD_DIM), (0, 2, 1, 3))
        v_diff = jnp.transpose(diff_b[..., dkw:].reshape(bp, s, DIFF_KVH, 2 * HEAD_DIM), (0, 2, 1, 3))
        o_diff = _flash("diff", q_diff, k_diff, v_ones(v_diff, 2 * LANES), near_diff, (lam_par, subln),
                        dv=2 * HEAD_DIM, out_width=bw, lam_init=lam_init)
        flat = lambda a: a.reshape(bp * s, a.shape[-1])
        xp = _merge([flat(o_cw), flat(o_sel)], flat(o_fox), flat(o_diff), mg, xp, w_br[l], w_o[l])
        xp = _ffn(xp, norm_ffn[l].reshape(1, d_model), w_gu[l], w_dn[l])
        st['cmp_p'].append(cmp_r.reshape(bp, s, 2, NSA_KVH, HEAD_DIM))
        st['sel_p'].append(sel_r.reshape(bp, s, 2, NSA_KVH, HEAD_DIM))
        st['win_p'].append(b3(win_r)[:, s - min(WINDOW, s):].reshape(bp, min(WINDOW, s), 2, NSA_KVH, HEAD_DIM))
        st['fkv_p'].append(fox_rw.reshape(bp, s, 2, FOX_KVH, HEAD_DIM))
        st['flf_p'].append(small3[..., n_gate:n_gate + fox_h])
        st['dkv_p'].append(diff_rw.reshape(bp, s, 2, DIFF_KVH, 2 * HEAD_DIM))

        nq, cmp_r, sel_r, win_r, fq, fox_rw, dq, diff_rw, mg, small = _project(xs, *proj_args)
        s3 = lambda a: a.reshape(bs, t_new, a.shape[-1])
        small3 = s3(small)
        gates_s = jnp.pad(small3[..., :n_gate].reshape(bs, rows_n, 3), ((0, 0), (0, 0), (0, LANES - 3)))
        new_chunk = jnp.pad(cmp_r.reshape(bs, 1, t_new * 2 * kvw),
                            ((0, 0), (0, SUBLANES - 1), (0, (D_CMP - t_new) * 2 * kvw)))
        tok_s = _compress_sample(pool_cmp, l, page_table, new_chunk, pe_c1[l], pe_c2[l], w_c1[l], w_c2[l])
        qn_bd = jnp.einsum('btgrd,gh->btgrhd', s3(nq).reshape(bs, t_new, NSA_KVH, nsa_r, HEAD_DIM), eye_n
                           ).reshape(bs, rows_n, kvw)
        o_c_s, bias_sel = _samp_cmp(qn_bd, tok_s, cb_s, ov_s, rsum_s, cur_s, expand_s, t5_sel_s, gates_s, n_top_s)
        o_s_s = _samp_attn("sel", qn_bd, pool_sel, l, page_table, s3(sel_r), bias_sel, (gates_s,),
                           dk=kvw, dv=HEAD_DIM, n_groups=NSA_KVH, layout="transposed")
        o_w_s = _samp_attn("win", qn_bd, win_state[l], l, None, s3(win_r), wb_s, (gates_s,),
                           dk=kvw, dv=HEAD_DIM, n_groups=NSA_KVH)
        logf_new_t = jnp.pad(jnp.transpose(small3[..., n_gate:n_gate + fox_h], (0, 2, 1)),
                             ((0, 0), (0, 0), (0, PAGE - t_new)))
        bias_fox = _samp_fox_bias(pool_logf_t, l, page_table, logf_new_t)
        fq_bd = jnp.einsum('btgrd,gh->btgrhd', s3(fq).reshape(bs, t_new, FOX_KVH, fox_r, HEAD_DIM), eye_f
                           ).reshape(bs, t_new * fox_h, FOX_KVH * HEAD_DIM)
        o_f_s = _samp_attn("fox", fq_bd, pool_fox, l, page_table, s3(fox_rw), bias_fox, (),
                           dk=FOX_KVH * HEAD_DIM, dv=HEAD_DIM, n_groups=FOX_KVH, layout="transposed")
        dq_s = jnp.transpose(s3(dq).reshape(bs, t_new, DIFF_KVH, diff_r, 2, HEAD_DIM), (0, 4, 1, 2, 3, 5))
        dq_bd = jnp.einsum('bctgrd,gch->bctgrhd', dq_s, eye_d.reshape(DIFF_KVH, 2, DIFF_KVH * 2)
                           ).reshape(bs, 2 * t_new * diff_h, DIFF_KVH * 2 * HEAD_DIM)
        o_d_s = _samp_attn("diff", dq_bd, pool_diff, l, page_table, s3(diff_rw), t5_diff_s, (lam_par, subln),
                           dk=DIFF_KVH * 2 * HEAD_DIM, dv=2 * HEAD_DIM, n_groups=DIFF_KVH, layout="interleaved",
                           lam_init=lam_init)
        flat_s = lambda a: a.reshape(bs * t_new, bw)
        xs = _merge([flat_s(o_c_s), flat_s(o_s_s), flat_s(o_w_s)], flat_s(o_f_s), flat_s(o_d_s), mg, xs,
                    w_br[l], w_o[l])
        xs = _ffn(xs, norm_ffn[l].reshape(1, d_model), w_gu[l], w_dn[l])
        win_full = jnp.concatenate([win_state[l], s3(win_r)], axis=1)
        st['cmp_s'].append(cmp_r.reshape(bs, t_new, 2, NSA_KVH, HEAD_DIM))
        st['sel_s'].append(sel_r.reshape(bs, t_new, 2, NSA_KVH, HEAD_DIM))
        st['win_s'].append(win_full[:, t_new:].reshape(bs, win_buf, 2, NSA_KVH, HEAD_DIM))
        st['fkv_s'].append(fox_rw.reshape(bs, t_new, 2, FOX_KVH, HEAD_DIM))
        st['flf_s'].append(small3[..., n_gate:n_gate + fox_h])
        st['dkv_s'].append(diff_rw.reshape(bs, t_new, 2, DIFF_KVH, 2 * HEAD_DIM))

    ns = {n: jnp.stack(v, axis=0) for n, v in st.items()}
    return (xp.reshape(bp, s, d_model), xs.reshape(bs, t_new, d_model),
            ns['cmp_p'], ns['cmp_s'], ns['sel_p'], ns['sel_s'], ns['win_p'], ns['win_s'],
            ns['fkv_p'], ns['fkv_s'], ns['flf_p'], ns['flf_s'], ns['dkv_p'], ns['dkv_s'])
```
